```python
import math
import jax, jax.numpy as jnp
from jax import lax
import numpy as np

D_MODEL = 2048
BATCH = 2
SEQ = 4096
DEPTH = 2

N_MIXERS = 2
MEM_LEN = 256
CHUNK = 128
A_HEAD_DIM = 128
A_WIDTH = D_MODEL
A_HEADS = A_WIDTH // A_HEAD_DIM
B_GROUP = 16
B_STATE = 64
B_WIDTH = D_MODEL
B_GROUPS = B_WIDTH // B_GROUP
DT_MIN = 1e-3
DT_MAX = 1e-1
LAM_RE_MAX = -1e-4
X_HEADS = 4
X_HEAD_DIM = D_MODEL // X_HEADS
D_FF = 4 * D_MODEL
EPS = 1e-6

kernel_name = "hybrid_gmlp_s5_memory_trunk"

F32 = jnp.float32


def rms_norm(x, g):
    xf = x.astype(F32)
    y = xf * lax.rsqrt(jnp.mean(xf * xf, axis=-1, keepdims=True) + EPS)
    return (y * g.astype(F32)).astype(x.dtype)


def layer_norm(x, g, b):
    xf = x.astype(F32)
    mu = jnp.mean(xf, axis=-1, keepdims=True)
    var = jnp.mean(jnp.square(xf - mu), axis=-1, keepdims=True)
    y = (xf - mu) * lax.rsqrt(var + EPS)
    return (y * g.astype(F32) + b.astype(F32)).astype(x.dtype)


def chunked_gmlp(h, w_in, b_in, ln_g, ln_b, w_s, b_s, w_out):
    bsz, seq, _ = h.shape
    z = jax.nn.gelu(h @ w_in + b_in, approximate=False)
    u, v = jnp.split(z, 2, axis=-1)
    v = layer_norm(v, ln_g, ln_b)
    v = v.reshape(bsz, seq // CHUNK, CHUNK, A_HEADS, A_HEAD_DIM)
    causal = jnp.tril(jnp.ones((CHUNK, CHUNK), dtype=bool))
    w = jnp.where(causal[None], w_s, jnp.zeros((), w_s.dtype))
    s = jnp.einsum('hts,bnshd->bnthd', w, v) + jnp.swapaxes(b_s, 0, 1)[None, None, :, :, None]
    s = s.reshape(bsz, seq, A_WIDTH)
    return (u * s) @ w_out


def s5_glu(h, w_in, lam_re, lam_im, log_dt, bm_re, bm_im, cm_re, cm_im, d_skip, w_out, b_out):
    bsz, seq, _ = h.shape
    u = (h @ w_in).astype(F32).reshape(bsz, seq, B_GROUPS, B_GROUP)
    lam = lax.complex(jnp.minimum(lam_re.astype(F32), LAM_RE_MAX), lam_im.astype(F32))
    dt = jnp.exp(log_dt.astype(F32))[:, None]
    lam_bar = jnp.exp(lam * dt)
    bmat = lax.complex(bm_re.astype(F32), bm_im.astype(F32))
    b_bar = ((lam_bar - 1.0) / lam)[..., None] * bmat
    bu = jnp.einsum('gpc,bsgc->sbgp', b_bar, u.astype(jnp.complex64))
    a = jnp.broadcast_to(lam_bar[None, None], (seq, 1, B_GROUPS, B_STATE))

    def combine(left, right):
        a_l, b_l = left
        a_r, b_r = right
        return a_r * a_l, a_r * b_l + b_r

    _, states = lax.associative_scan(combine, (a, bu), axis=0)
    cmat = lax.complex(cm_re.astype(F32), cm_im.astype(F32))
    y = jnp.einsum('gcp,sbgp->bsgc', cmat, states).real + d_skip.astype(F32) * u
    y = jax.nn.gelu(y.reshape(bsz, seq, B_WIDTH), approximate=False).astype(h.dtype)
    val, gate = jnp.split(y @ w_out + b_out, 2, axis=-1)
    return val * jax.nn.sigmoid(gate)


def memory_attention(h, mem_n, w_q, w_kv, w_o):
    bsz, seq, _ = h.shape
    m = mem_n.shape[1]
    q = (h @ w_q).reshape(bsz, seq, X_HEADS, X_HEAD_DIM)
    k, v = jnp.split(mem_n @ w_kv, 2, axis=-1)
    k = k.reshape(bsz, m, X_HEADS, X_HEAD_DIM)
    v = v.reshape(bsz, m, X_HEADS, X_HEAD_DIM)
    scores = jnp.einsum('bshd,bmhd->bhsm', q, k).astype(F32) * (X_HEAD_DIM ** -0.5)
    p = jax.nn.softmax(scores, axis=-1).astype(v.dtype)
    o = jnp.einsum('bhsm,bmhd->bshd', p, v).reshape(bsz, seq, X_HEADS * X_HEAD_DIM)
    return o @ w_o


def squared_relu_mlp(h, w_up, w_down):
    return jnp.square(jax.nn.relu(h @ w_up)) @ w_down


def setup_inputs(seed: int = 0) -> dict:
    key = jax.random.key(seed)
    ks = iter(jax.random.split(key, 64))

    def nrm(shape, scale):
        return scale * jax.random.normal(next(ks), shape, F32)

    n_a = len(range(0, DEPTH, N_MIXERS))
    n_b = len(range(1, DEPTH, N_MIXERS))
    x = nrm((BATCH, SEQ, D_MODEL), 1.0)
    mem = nrm((BATCH, MEM_LEN, D_MODEL), 1.0)
    g_mix = 1.0 + nrm((DEPTH, D_MODEL), 0.02)
    g_xattn = 1.0 + nrm((DEPTH, D_MODEL), 0.02)
    g_mem = 1.0 + nrm((DEPTH, D_MODEL), 0.02)
    g_ff = 1.0 + nrm((DEPTH, D_MODEL), 0.02)
    g_final = 1.0 + nrm((D_MODEL,), 0.02)
    a_w_in = nrm((n_a, D_MODEL, 2 * A_WIDTH), D_MODEL ** -0.5)
    a_b_in = nrm((n_a, 2 * A_WIDTH), 0.02)
    a_ln_g = 1.0 + nrm((n_a, A_WIDTH), 0.02)
    a_ln_b = nrm((n_a, A_WIDTH), 0.02)
    a_w_s = nrm((n_a, A_HEADS, CHUNK, CHUNK), 0.5 * CHUNK ** -0.5)
    a_b_s = 1.0 + nrm((n_a, A_HEADS, CHUNK), 0.1)
    a_w_out = nrm((n_a, A_WIDTH, D_MODEL), A_WIDTH ** -0.5)
    b_w_in = nrm((n_b, D_MODEL, B_WIDTH), D_MODEL ** -0.5)
    n_idx = jnp.arange(B_STATE, dtype=F32)
    b_lam_re = -0.5 + nrm((n_b, B_GROUPS, B_STATE), 0.01)
    b_lam_im = math.pi * n_idx + nrm((n_b, B_GROUPS, B_STATE), 0.01)
    b_log_dt = jax.random.uniform(next(ks), (n_b, B_GROUPS), F32, math.log(DT_MIN), math.log(DT_MAX))
    b_bm_re = nrm((n_b, B_GROUPS, B_STATE, B_GROUP), (2 * B_GROUP) ** -0.5)
    b_bm_im = nrm((n_b, B_GROUPS, B_STATE, B_GROUP), (2 * B_GROUP) ** -0.5)
    b_cm_re = nrm((n_b, B_GROUPS, B_GROUP, B_STATE), B_STATE ** -0.5)
    b_cm_im = nrm((n_b, B_GROUPS, B_GROUP, B_STATE), B_STATE ** -0.5)
    b_d = nrm((n_b, B_GROUPS, B_GROUP), 1.0)
    b_w_out = nrm((n_b, B_WIDTH, 2 * D_MODEL), B_WIDTH ** -0.5)
    b_b_out = nrm((n_b, 2 * D_MODEL), 0.02)
    x_w_q = nrm((DEPTH, D_MODEL, D_MODEL), D_MODEL ** -0.5)
    x_w_kv = nrm((DEPTH, D_MODEL, 2 * D_MODEL), D_MODEL ** -0.5)
    x_w_o = nrm((DEPTH, D_MODEL, D_MODEL), D_MODEL ** -0.5)
    f_w_up = nrm((DEPTH, D_MODEL, D_FF), D_MODEL ** -0.5)
    f_w_down = nrm((DEPTH, D_FF, D_MODEL), D_FF ** -0.5)
    return {
        'x': x, 'mem': mem,
        'g_mix': g_mix, 'g_xattn': g_xattn, 'g_mem': g_mem, 'g_ff': g_ff, 'g_final': g_final,
        'a_w_in': a_w_in, 'a_b_in': a_b_in, 'a_ln_g': a_ln_g, 'a_ln_b': a_ln_b,
        'a_w_s': a_w_s, 'a_b_s': a_b_s, 'a_w_out': a_w_out,
        'b_w_in': b_w_in, 'b_lam_re': b_lam_re, 'b_lam_im': b_lam_im, 'b_log_dt': b_log_dt,
        'b_bm_re': b_bm_re, 'b_bm_im': b_bm_im, 'b_cm_re': b_cm_re, 'b_cm_im': b_cm_im,
        'b_d': b_d, 'b_w_out': b_w_out, 'b_b_out': b_b_out,
        'x_w_q': x_w_q, 'x_w_kv': x_w_kv, 'x_w_o': x_w_o,
        'f_w_up': f_w_up, 'f_w_down': f_w_down,
    }


def reference(x, mem, g_mix, g_xattn, g_mem, g_ff, g_final,
              a_w_in, a_b_in, a_ln_g, a_ln_b, a_w_s, a_b_s, a_w_out,
              b_w_in, b_lam_re, b_lam_im, b_log_dt, b_bm_re, b_bm_im, b_cm_re, b_cm_im,
              b_d, b_w_out, b_b_out,
              x_w_q, x_w_kv, x_w_o, f_w_up, f_w_down):
    for i in range(DEPTH):
        j = i // N_MIXERS
        h = rms_norm(x, g_mix[i])
        if i % N_MIXERS == 0:
            x = x + chunked_gmlp(h, a_w_in[j], a_b_in[j], a_ln_g[j], a_ln_b[j],
                                 a_w_s[j], a_b_s[j], a_w_out[j])
        else:
            x = x + s5_glu(h, b_w_in[j], b_lam_re[j], b_lam_im[j], b_log_dt[j],
                           b_bm_re[j], b_bm_im[j], b_cm_re[j], b_cm_im[j],
                           b_d[j], b_w_out[j], b_b_out[j])
        x = x + memory_attention(rms_norm(x, g_xattn[i]), rms_norm(mem, g_mem[i]),
                                 x_w_q[i], x_w_kv[i], x_w_o[i])
        x = x + squared_relu_mlp(rms_norm(x, g_ff[i]), f_w_up[i], f_w_down[i])
    return rms_norm(x, g_final)
```

```python
import functools
import math

import jax
import jax.numpy as jnp
from jax import lax
from jax.experimental import pallas as pl
from jax.experimental.pallas import tpu as pltpu

F32 = jnp.float32
BF16 = jnp.bfloat16

EPS = 1e-6
LAM_RE_MAX = -1e-4
N_MIXERS = 2

GMLP_CHUNK = 128
GMLP_HEAD_DIM = 128
ATTN_HEADS = 4
S5_GROUP = 16
S5_STATE = 64
S5_CHUNK = 16
S5_BLOCK = S5_CHUNK * S5_GROUP

V7X_VMEM_LIMIT_BYTES = 56 * 1024 * 1024


def _params(semantics):
    return pltpu.CompilerParams(dimension_semantics=semantics, vmem_limit_bytes=V7X_VMEM_LIMIT_BYTES)


def _resident(block_shape, index_map):
    return pl.BlockSpec(block_shape, index_map, pipeline_mode=pl.Buffered(1))


def _tile(n, want):
    t = min(n, want)
    assert n % t == 0, (n, want)
    return t


def _rms(x, g):
    ms = jnp.mean(x * x, axis=-1, keepdims=True)
    return x * lax.rsqrt(ms + EPS) * g


def _gelu(x):
    return 0.5 * x * (1.0 + lax.erf(x * (2.0 ** -0.5)))


def _nt_dot(a, b):
    return lax.dot_general(a, b, (((1,), (1,)), ((), ())), preferred_element_type=F32)


def _gmlp_in_kernel(x_ref, g_ref, w_ref, b_ref, lng_ref, lnb_ref, o_ref):
    h = _rms(x_ref[...], g_ref[...]).astype(BF16)
    z = _gelu(jnp.dot(h, w_ref[...], preferred_element_type=F32) + b_ref[...])

    @pl.when(pl.program_id(0) == 0)
    def _():
        o_ref[...] = z.astype(o_ref.dtype)

    @pl.when(pl.program_id(0) == 1)
    def _():
        mu = jnp.mean(z, axis=-1, keepdims=True)
        zc = z - mu
        var = jnp.mean(zc * zc, axis=-1, keepdims=True)
        o_ref[...] = (zc * lax.rsqrt(var + EPS) * lng_ref[...] + lnb_ref[...]).astype(o_ref.dtype)


def _gmlp_in(x, g, w, b, ln_g, ln_b):
    n, d = x.shape
    width = w.shape[1] // 2
    tm = _tile(n, 512)
    row = lambda a: a.reshape(1, -1)
    return pl.pallas_call(
        _gmlp_in_kernel,
        grid=(2, n // tm),
        in_specs=[
            pl.BlockSpec((tm, d), lambda j, i: (i, 0)),
            pl.BlockSpec((1, d), lambda j, i: (0, 0)),
            pl.BlockSpec((d, width), lambda j, i: (0, j)),
            pl.BlockSpec((1, width), lambda j, i: (0, j)),
            pl.BlockSpec((1, width), lambda j, i: (0, 0)),
            pl.BlockSpec((1, width), lambda j, i: (0, 0)),
        ],
        out_specs=pl.BlockSpec((tm, width), lambda j, i: (i, j)),
        out_shape=jax.ShapeDtypeStruct((n, 2 * width), BF16),
        compiler_params=_params(("arbitrary", "arbitrary")),
        name="gmlp_in",
    )(x, row(g), w, row(b), row(ln_g), row(ln_b))


def _gmlp_out_kernel(x_ref, u_ref, v_ref, ws_ref, bs_ref, w_ref, o_ref, gated_ref):
    tm, width = u_ref.shape
    heads = ws_ref.shape[0]
    t_idx = lax.broadcasted_iota(jnp.int32, (GMLP_CHUNK, GMLP_CHUNK), 0)
    s_idx = lax.broadcasted_iota(jnp.int32, (GMLP_CHUNK, GMLP_CHUNK), 1)
    causal = t_idx >= s_idx
    for h in range(heads):
        w_h = jnp.where(causal, ws_ref[h], 0.0).astype(BF16)
        b_h = bs_ref[h]
        cols = slice(h * GMLP_HEAD_DIM, (h + 1) * GMLP_HEAD_DIM)
        for c in range(tm // GMLP_CHUNK):
            rows = slice(c * GMLP_CHUNK, (c + 1) * GMLP_CHUNK)
            s = jnp.dot(w_h, v_ref[rows, cols], preferred_element_type=F32) + b_h
            gated_ref[rows, cols] = (u_ref[rows, cols].astype(F32) * s).astype(BF16)
    o_ref[...] = x_ref[...] + jnp.dot(gated_ref[...], w_ref[...], preferred_element_type=F32)


def _gmlp_out(x, z, w_s, b_s, w_out):
    n, d = x.shape
    width = z.shape[1] // 2
    heads = w_s.shape[0]
    tm = _tile(n, 512)
    return pl.pallas_call(
        _gmlp_out_kernel,
        grid=(n // tm,),
        in_specs=[
            pl.BlockSpec((tm, d), lambda i: (i, 0)),
            pl.BlockSpec((tm, width), lambda i: (i, 0)),
            pl.BlockSpec((tm, width), lambda i: (i, 1)),
            _resident((heads, GMLP_CHUNK, GMLP_CHUNK), lambda i: (0, 0, 0)),
            _resident((heads, GMLP_CHUNK, 1), lambda i: (0, 0, 0)),
            _resident((width, d), lambda i: (0, 0)),
        ],
        out_specs=pl.BlockSpec((tm, d), lambda i: (i, 0)),
        out_shape=jax.ShapeDtypeStruct((n, d), F32),
        scratch_shapes=[pltpu.VMEM((tm, width), BF16)],
        compiler_params=_params(("arbitrary",)),
        name="gmlp_out",
    )(x, z, z, w_s, b_s.reshape(heads, GMLP_CHUNK, 1), w_out)


def _norm_matmul_kernel(x_ref, g_ref, w_ref, o_ref):
    h = _rms(x_ref[...], g_ref[...]).astype(BF16)
    o_ref[...] = jnp.dot(h, w_ref[...], preferred_element_type=F32).astype(o_ref.dtype)


def _mem_kv(mem, g, w_kv):
    n, d = mem.shape
    m = w_kv.shape[1]
    tn = _tile(m, 2048)
    return pl.pallas_call(
        _norm_matmul_kernel,
        grid=(m // tn,),
        in_specs=[
            _resident((n, d), lambda j: (0, 0)),
            _resident((1, d), lambda j: (0, 0)),
            pl.BlockSpec((d, tn), lambda j: (0, j)),
        ],
        out_specs=pl.BlockSpec((n, tn), lambda j: (0, j)),
        out_shape=jax.ShapeDtypeStruct((n, m), BF16),
        compiler_params=_params(("arbitrary",)),
        name="mem_kv",
    )(mem, g.reshape(1, -1), w_kv)


def _attn_kernel(x_ref, g_ref, wq_ref, kv_ref, wo_ref, o_ref, heads_ref):
    x = x_ref[...]
    d = x.shape[1]
    hd = d // ATTN_HEADS
    q = jnp.dot(_rms(x, g_ref[...]).astype(BF16), wq_ref[...], preferred_element_type=F32).astype(BF16)
    scale = hd ** -0.5
    for h in range(ATTN_HEADS):
        cols = slice(h * hd, (h + 1) * hd)
        k_h = kv_ref[:, h * hd:(h + 1) * hd]
        v_h = kv_ref[:, d + h * hd:d + (h + 1) * hd]
        s = _nt_dot(q[:, cols], k_h) * scale
        p = jnp.exp(s - jnp.max(s, axis=-1, keepdims=True))
        p = p / jnp.sum(p, axis=-1, keepdims=True)
        heads_ref[:, cols] = jnp.dot(p.astype(BF16), v_h, preferred_element_type=F32).astype(BF16)
    o_ref[...] = x + jnp.dot(heads_ref[...], wo_ref[...], preferred_element_type=F32)


def _attention(x, g, w_q, kv, w_o, seq):
    n, d = x.shape
    mem_len = kv.shape[0] // (n // seq)
    tm = _tile(seq, 512)
    per_batch = seq // tm
    return pl.pallas_call(
        _attn_kernel,
        grid=(n // tm,),
        in_specs=[
            pl.BlockSpec((tm, d), lambda i: (i, 0)),
            _resident((1, d), lambda i: (0, 0)),
            _resident((d, d), lambda i: (0, 0)),
            pl.BlockSpec((mem_len, 2 * d), lambda i: (i // per_batch, 0)),
            _resident((d, d), lambda i: (0, 0)),
        ],
        out_specs=pl.BlockSpec((tm, d), lambda i: (i, 0)),
        out_shape=jax.ShapeDtypeStruct((n, d), F32),
        scratch_shapes=[pltpu.VMEM((tm, d), BF16)],
        compiler_params=_params(("arbitrary",)),
        name="attention",
    )(x, g.reshape(1, -1), w_q, kv, w_o)


def _mlp_kernel(x_ref, g_ref, gf_ref, wup_ref, wdn_ref, o_ref, h_ref, *, final_norm):
    j = pl.program_id(1)

    @pl.when(j == 0)
    def _():
        x = x_ref[...]
        h_ref[...] = _rms(x, g_ref[...]).astype(BF16)
        o_ref[...] = x

    a = jnp.dot(h_ref[...], wup_ref[...], preferred_element_type=F32)
    a = jnp.square(jnp.maximum(a, 0.0)).astype(BF16)
    o_ref[...] += jnp.dot(a, wdn_ref[...], preferred_element_type=F32)

    if final_norm:
        @pl.when(j == pl.num_programs(1) - 1)
        def _():
            o_ref[...] = _rms(o_ref[...], gf_ref[...])


def _mlp(x, g, w_up, w_down, g_final=None):
    n, d = x.shape
    ff = w_up.shape[1]
    tm = _tile(n, 1024)
    tf = _tile(ff, 512)
    gf = g if g_final is None else g_final
    return pl.pallas_call(
        functools.partial(_mlp_kernel, final_norm=g_final is not None),
        grid=(n // tm, ff // tf),
        in_specs=[
            pl.BlockSpec((tm, d), lambda i, j: (i, 0)),
            pl.BlockSpec((1, d), lambda i, j: (0, 0)),
            pl.BlockSpec((1, d), lambda i, j: (0, 0)),
            pl.BlockSpec((d, tf), lambda i, j: (0, j)),
            pl.BlockSpec((tf, d), lambda i, j: (j, 0)),
        ],
        out_specs=pl.BlockSpec((tm, d), lambda i, j: (i, 0)),
        out_shape=jax.ShapeDtypeStruct((n, d), F32),
        scratch_shapes=[pltpu.VMEM((tm, d), BF16)],
        compiler_params=_params(("arbitrary", "arbitrary")),
        name="mlp",
    )(x, g.reshape(1, -1), gf.reshape(1, -1), w_up, w_down)


def _s5_in_kernel(x_ref, g_ref, wt_ref, o_ref):
    h = _rms(x_ref[...], g_ref[...]).astype(BF16)
    ut = _nt_dot(wt_ref[...], h)
    o_ref[...] = ut.astype(BF16).reshape(o_ref.shape)


def _s5_in(xv, g, w_in_t):
    bsz, r, cd = xv.shape
    width, d = w_in_t.shape
    groups = width // S5_GROUP
    return pl.pallas_call(
        _s5_in_kernel,
        grid=(bsz, S5_CHUNK),
        in_specs=[
            pl.BlockSpec((None, r, d), lambda b, t: (b, 0, t)),
            _resident((1, d), lambda b, t: (0, 0)),
            _resident((width, d), lambda b, t: (0, 0)),
        ],
        out_specs=pl.BlockSpec((None, groups, None, S5_GROUP, r), lambda b, t: (b, 0, t, 0, 0)),
        out_shape=jax.ShapeDtypeStruct((bsz, groups, S5_CHUNK, S5_GROUP, r), BF16),
        compiler_params=_params(("arbitrary", "arbitrary")),
        name="s5_in",
    )(xv, g.reshape(1, -1), w_in_t)


def _cpow(zr, zi, e):
    mag = jnp.exp(e * zr)
    return mag * jnp.cos(e * zi), mag * jnp.sin(e * zi)


def _s5_prep_kernel(lrc_ref, lic_ref, lrr_ref, lir_ref, ldt_ref, bre_ref, bim_ref, cre_ref, cim_ref,
                    crp_ref, cip_ref, dsk_ref, t_ref, wb_ref, wcr_ref, wci_ref, dre_ref, dim_ref):
    gb = t_ref.shape[0]
    hi = lax.Precision.HIGHEST
    lane = lax.broadcasted_iota(jnp.int32, (S5_GROUP, S5_BLOCK), 1)
    row = lax.broadcasted_iota(jnp.int32, (S5_GROUP, S5_BLOCK), 0)
    skip_mask = (lane // S5_GROUP == S5_CHUNK - 1) & (lane % S5_GROUP == row)
    e_row = (S5_CHUNK - 1 - lax.broadcasted_iota(jnp.int32, (1, S5_BLOCK), 1) // S5_GROUP).astype(F32)
    e_col = (lax.broadcasted_iota(jnp.int32, (S5_BLOCK, 1), 0) // S5_GROUP + 1).astype(F32)
    for g in range(gb):
        dt = jnp.exp(ldt_ref[g])
        lr = jnp.minimum(lrc_ref[g], LAM_RE_MAX)
        li = lic_ref[g]
        zr, zi = lr * dt, li * dt
        ar, ai = _cpow(zr, zi, 1.0)
        den = lr * lr + li * li
        qr = ((ar - 1.0) * lr + ai * li) / den
        qi = (ai * lr - (ar - 1.0) * li) / den
        bbr = qr * bre_ref[g] - qi * bim_ref[g]
        bbi = qr * bim_ref[g] + qi * bre_ref[g]
        pr, pi_ = _cpow(zr, zi, e_row)
        wbr = pr * bbr - pi_ * bbi
        wbi = pr * bbi + pi_ * bbr
        wb_ref[g] = jnp.concatenate([wbr, wbi], axis=0).astype(BF16)
        krev = (jnp.dot(cre_ref[g], wbr, preferred_element_type=F32, precision=hi)
                - jnp.dot(cim_ref[g], wbi, preferred_element_type=F32, precision=hi))
        krev = krev + jnp.where(skip_mask, dsk_ref[g], 0.0)
        kext = jnp.concatenate([krev, jnp.zeros_like(krev)], axis=1)
        for tp in range(S5_CHUNK):
            off = (S5_CHUNK - 1 - tp) * S5_GROUP
            win = kext if off == 0 else pltpu.roll(kext, 2 * S5_BLOCK - off, axis=1)
            t_ref[g, tp * S5_GROUP:(tp + 1) * S5_GROUP, :] = win[:, :S5_BLOCK].astype(BF16)
        lr2 = jnp.minimum(lrr_ref[g], LAM_RE_MAX)
        zr2, zi2 = lr2 * dt, lir_ref[g] * dt
        cr, ci = _cpow(zr2, zi2, e_col)
        wcr_ref[g] = (crp_ref[g] * cr - cip_ref[g] * ci).astype(BF16)
        wci_ref[g] = (-(crp_ref[g] * ci + cip_ref[g] * cr)).astype(BF16)
        dr, di = _cpow(zr2, zi2, float(S5_CHUNK))
        dre_ref[g] = dr
        dim_ref[g] = di


def _s5_prep(lam_re, lam_im, log_dt, bm_re, bm_im, cm_re, cm_im, d_skip):
    groups, state = lam_re.shape
    gb = _tile(groups, 8)
    pair = lambda a: jnp.broadcast_to(a.reshape(groups // 2, 1, 1, 2 * state),
                                      (groups // 2, 2, 1, 2 * state)).reshape(groups, 1, 2 * state)
    tile_t = lambda a: jnp.tile(a, (1, 1, S5_CHUNK))
    half = jnp.eye(2, dtype=F32)[jnp.arange(groups) % 2]
    cpad = lambda a: jnp.tile((a[:, :, None, :] * half[:, None, :, None]).reshape(groups, S5_GROUP, 2 * state),
                              (1, S5_CHUNK, 1))
    g3 = lambda shape: pl.BlockSpec((gb,) + shape, lambda i: (i, 0, 0))
    outs = pl.pallas_call(
        _s5_prep_kernel,
        grid=(groups // gb,),
        in_specs=[g3((state, 1)), g3((state, 1)), g3((1, 2 * state)), g3((1, 2 * state)), g3((1, 1)),
                  g3((state, S5_BLOCK)), g3((state, S5_BLOCK)), g3((S5_GROUP, state)), g3((S5_GROUP, state)),
                  g3((S5_BLOCK, 2 * state)), g3((S5_BLOCK, 2 * state)), g3((1, S5_BLOCK))],
        out_specs=[g3((S5_BLOCK, S5_BLOCK)), g3((2 * state, S5_BLOCK)), g3((S5_BLOCK, 2 * state)),
                   g3((S5_BLOCK, 2 * state)), g3((1, 2 * state)), g3((1, 2 * state))],
        out_shape=[jax.ShapeDtypeStruct((groups, S5_BLOCK, S5_BLOCK), BF16),
                   jax.ShapeDtypeStruct((groups, 2 * state, S5_BLOCK), BF16),
                   jax.ShapeDtypeStruct((groups, S5_BLOCK, 2 * state), BF16),
                   jax.ShapeDtypeStruct((groups, S5_BLOCK, 2 * state), BF16),
                   jax.ShapeDtypeStruct((groups, 1, 2 * state), F32),
                   jax.ShapeDtypeStruct((groups, 1, 2 * state), F32)],
        compiler_params=_params(("arbitrary",)),
        name="s5_prep",
    )(lam_re.reshape(groups, state, 1), lam_im.reshape(groups, state, 1), pair(lam_re), pair(lam_im),
      log_dt.reshape(groups, 1, 1), tile_t(bm_re), tile_t(bm_im), cm_re, cm_im, cpad(cm_re), cpad(cm_im),
      jnp.tile(d_skip.reshape(groups, 1, S5_GROUP), (1, 1, S5_CHUNK)))
    return outs


def _s5_core_kernel(ut_ref, t_ref, wb_ref, wcr_ref, wci_ref, dre_ref, dim_ref, o_ref,
                    elr_ref, eli_ref, epr_ref, epi_ref):
    gb, _, r = ut_ref.shape
    state = wb_ref.shape[1] // 2
    for p in range(gb // 2):
        parts = [jnp.dot(wb_ref[2 * p + k], ut_ref[2 * p + k], preferred_element_type=F32) for k in range(2)]
        lanes = slice(p * 2 * state, (p + 1) * 2 * state)
        elr_ref[:, lanes] = jnp.concatenate([parts[0][:state], parts[1][:state]], axis=0).T
        eli_ref[:, lanes] = jnp.concatenate([parts[0][state:], parts[1][state:]], axis=0).T
    first = lax.broadcasted_iota(jnp.int32, (1, 2 * state), 1) < state
    pair_row = lambda ref, p: jnp.where(first, ref[2 * p], ref[2 * p + 1])
    dre = jnp.concatenate([pair_row(dre_ref, p) for p in range(gb // 2)], axis=1)
    dim = jnp.concatenate([pair_row(dim_ref, p) for p in range(gb // 2)], axis=1)

    def step(i, carry):
        er, ei = carry
        epr_ref[pl.ds(i, 1), :] = er
        epi_ref[pl.ds(i, 1), :] = ei
        lr = elr_ref[pl.ds(i, 1), :]
        li = eli_ref[pl.ds(i, 1), :]
        return dre * er - dim * ei + lr, dre * ei + dim * er + li

    zero = jnp.zeros((1, gb * state), F32)
    lax.fori_loop(0, r, step, (zero, zero))
    for g in range(gb):
        lanes = slice((g // 2) * 2 * state, (g // 2 + 1) * 2 * state)
        y = jnp.dot(t_ref[g], ut_ref[g], preferred_element_type=F32)
        y = y + _nt_dot(wcr_ref[g], epr_ref[:, lanes].astype(BF16))
        y = y + _nt_dot(wci_ref[g], epi_ref[:, lanes].astype(BF16))
        o_ref[g] = _gelu(y).astype(BF16)


def _s5_core(ut, tmat, wb, wcr, wci, dre, dim):
    bsz, groups, blk, r = ut.shape
    state = wb.shape[1] // 2
    gb = _tile(groups, 8)
    g3 = lambda shape: pl.BlockSpec((gb,) + shape, lambda b, i: (i, 0, 0))
    return pl.pallas_call(
        _s5_core_kernel,
        grid=(bsz, groups // gb),
        in_specs=[pl.BlockSpec((None, gb, blk, r), lambda b, i: (b, i, 0, 0)),
                  g3((blk, blk)), g3((2 * state, blk)), g3((blk, 2 * state)), g3((blk, 2 * state)),
                  g3((1, 2 * state)), g3((1, 2 * state))],
        out_specs=pl.BlockSpec((None, gb, blk, r), lambda b, i: (b, i, 0, 0)),
        out_shape=jax.ShapeDtypeStruct(ut.shape, BF16),
        scratch_shapes=[pltpu.VMEM((r, gb * state), F32) for _ in range(4)],
        compiler_params=_params(("arbitrary", "arbitrary")),
        name="s5_core",
    )(ut, tmat, wb, wcr, wci, dre, dim)


def _s5_out_kernel(x_ref, y_ref, wv_ref, wg_ref, bv_ref, bg_ref, o_ref):
    groups, gsz, r = y_ref.shape
    y = y_ref[...].reshape(groups * gsz, r).astype(F32).T.astype(BF16)
    val = jnp.dot(y, wv_ref[...], preferred_element_type=F32) + bv_ref[...]
    gate = jnp.dot(y, wg_ref[...], preferred_element_type=F32) + bg_ref[...]
    o_ref[...] = x_ref[...] + val * jax.nn.sigmoid(gate)


def _s5_out(xv, yt, w_out, b_out):
    bsz, r, cd = xv.shape
    width, d2 = w_out.shape
    d = d2 // 2
    groups = width // S5_GROUP
    tn = _tile(d, 1024)
    nc = d // tn
    b2 = b_out.reshape(1, d2)
    return pl.pallas_call(
        _s5_out_kernel,
        grid=(nc, bsz, S5_CHUNK),
        in_specs=[
            pl.BlockSpec((None, r, tn), lambda c, b, t: (b, 0, t * nc + c)),
            pl.BlockSpec((None, groups, None, S5_GROUP, r), lambda c, b, t: (b, 0, t, 0, 0)),
            pl.BlockSpec((width, tn), lambda c, b, t: (0, c)),
            pl.BlockSpec((width, tn), lambda c, b, t: (0, nc + c)),
            pl.BlockSpec((1, tn), lambda c, b, t: (0, c)),
            pl.BlockSpec((1, tn), lambda c, b, t: (0, nc + c)),
        ],
        out_specs=pl.BlockSpec((None, r, tn), lambda c, b, t: (b, 0, t * nc + c)),
        out_shape=jax.ShapeDtypeStruct(xv.shape, F32),
        compiler_params=_params(("arbitrary", "arbitrary", "arbitrary")),
        name="s5_out",
    )(xv, yt, w_out, w_out, b2, b2)


def _s5_mixer(x, seq, g, w_in, lam_re, lam_im, log_dt, bm_re, bm_im, cm_re, cm_im, d_skip, w_out, b_out):
    n, d = x.shape
    bsz = n // seq
    r = seq // S5_CHUNK
    width = w_in.shape[1]
    groups = width // S5_GROUP
    xv = x.reshape(bsz, r, S5_CHUNK * d)
    ut = _s5_in(xv, g, w_in.T.astype(BF16))
    tmat, wb, wcr, wci, dre, dim = _s5_prep(lam_re, lam_im, log_dt, bm_re, bm_im, cm_re, cm_im, d_skip)
    yt = _s5_core(ut.reshape(bsz, groups, S5_BLOCK, r), tmat, wb, wcr, wci, dre, dim)
    out = _s5_out(xv, yt.reshape(bsz, groups, S5_CHUNK, S5_GROUP, r), w_out.astype(BF16), b_out)
    return out.reshape(n, d)


def kernel(x, mem, g_mix, g_xattn, g_mem, g_ff, g_final, a_w_in, a_b_in, a_ln_g, a_ln_b, a_w_s, a_b_s, a_w_out, b_w_in, b_lam_re, b_lam_im, b_log_dt, b_bm_re, b_bm_im, b_cm_re, b_cm_im, b_d, b_w_out, b_b_out, x_w_q, x_w_kv, x_w_o, f_w_up, f_w_down):
    bsz, seq, d = x.shape
    depth = g_mix.shape[0]
    assert seq % GMLP_CHUNK == 0 and seq % S5_CHUNK == 0 and d % ATTN_HEADS == 0
    h = x.reshape(bsz * seq, d)
    mem2 = mem.reshape(-1, d)
    for i in range(depth):
        j = i // N_MIXERS
        if i % N_MIXERS == 0:
            z = _gmlp_in(h, g_mix[i], a_w_in[j].astype(BF16), a_b_in[j], a_ln_g[j], a_ln_b[j])
            h = _gmlp_out(h, z, a_w_s[j], a_b_s[j], a_w_out[j].astype(BF16))
        else:
            h = _s5_mixer(h, seq, g_mix[i], b_w_in[j], b_lam_re[j], b_lam_im[j], b_log_dt[j], b_bm_re[j],
                          b_bm_im[j], b_cm_re[j], b_cm_im[j], b_d[j], b_w_out[j], b_b_out[j])
        kv = _mem_kv(mem2, g_mem[i], x_w_kv[i].astype(BF16))
        h = _attention(h, g_xattn[i], x_w_q[i].astype(BF16), kv, x_w_o[i].astype(BF16), seq)
        h = _mlp(h, g_ff[i], f_w_up[i].astype(BF16), f_w_down[i].astype(BF16),
                 g_final if i == depth - 1 else None)
    return h.reshape(bsz, seq, d)
```

```python
import functools

import jax
import jax.numpy as jnp
from jax import lax
from jax.experimental import pallas as pl
from jax.experimental.pallas import tpu as pltpu

F32 = jnp.float32
BF16 = jnp.bfloat16

EPS = 1e-6
LAM_RE_MAX = -1e-4
N_MIXERS = 2

GMLP_CHUNK = 128
GMLP_HEAD_DIM = 128
ATTN_HEADS = 4
S5_GROUP = 16
S5_STATE = 64
S5_CHUNK = 16
S5_BLOCK = S5_CHUNK * S5_GROUP
S5_PAIR = 2 * S5_STATE
S5_POW_ROWS = 24

STAGE_ROWS = 256
V7X_VMEM_LIMIT_BYTES = 60 * 1024 * 1024


def _params(*semantics):
    return pltpu.CompilerParams(dimension_semantics=semantics, vmem_limit_bytes=V7X_VMEM_LIMIT_BYTES)


def _once(block_shape, index_map):
    return pl.BlockSpec(block_shape, index_map, pipeline_mode=pl.Buffered(1))


def _tile(n, want):
    t = min(n, want)
    assert n % t == 0, (n, want)
    return t


def _rms(x, g):
    ms = jnp.mean(x * x, axis=-1, keepdims=True)
    return x * lax.rsqrt(ms + EPS) * g


def _gelu(x):
    return 0.5 * x * (1.0 + lax.erf(x * (2.0 ** -0.5)))


def _nt_dot(a, b):
    return lax.dot_general(a, b, (((1,), (1,)), ((), ())), preferred_element_type=F32)


def _row(a):
    return a.reshape(1, -1)


def _stage_spec(layer, k, ncols, col=0):
    last = k // STAGE_ROWS - 1
    return pl.BlockSpec((None, STAGE_ROWS, ncols), lambda *ids: (layer, jnp.minimum(ids[-1], last), col))


def _work_index(step, n_stage):
    return jnp.maximum(step - n_stage, 0)


def _stage_rows(step, src_ref, dst_ref):
    rows = pl.ds(pl.multiple_of(step * STAGE_ROWS, STAGE_ROWS), STAGE_ROWS)
    dst_ref[rows, :] = src_ref[...].astype(BF16)


def _gmlp_in_kernel(x_ref, g_ref, w_ref, b_ref, lng_ref, lnb_ref, o_ref, wb_ref, *, n_stage):
    s = pl.program_id(1)

    @pl.when(s < n_stage)
    def _():
        _stage_rows(s, w_ref, wb_ref)

    @pl.when(s >= n_stage)
    def _():
        h = _rms(x_ref[...], g_ref[...]).astype(BF16)
        z = _gelu(jnp.dot(h, wb_ref[...], preferred_element_type=F32) + b_ref[...])

        @pl.when(pl.program_id(0) == 0)
        def _():
            o_ref[...] = z.astype(o_ref.dtype)

        @pl.when(pl.program_id(0) == 1)
        def _():
            mu = jnp.mean(z, axis=-1, keepdims=True)
            zc = z - mu
            var = jnp.mean(zc * zc, axis=-1, keepdims=True)
            o_ref[...] = (zc * lax.rsqrt(var + EPS) * lng_ref[...] + lnb_ref[...]).astype(o_ref.dtype)


def _gmlp_in(x, g, w_all, layer, b, ln_g, ln_b):
    n, d = x.shape
    width = w_all.shape[2] // 2
    tm = _tile(n, 512)
    n_stage = d // STAGE_ROWS
    row_map = lambda j, s: (_work_index(s, n_stage), 0)
    return pl.pallas_call(
        functools.partial(_gmlp_in_kernel, n_stage=n_stage),
        grid=(2, n_stage + n // tm),
        in_specs=[
            pl.BlockSpec((tm, d), row_map),
            pl.BlockSpec((1, d), lambda j, s: (0, 0)),
            pl.BlockSpec((None, STAGE_ROWS, width), lambda j, s: (layer, jnp.minimum(s, n_stage - 1), j)),
            pl.BlockSpec((1, width), lambda j, s: (0, j)),
            pl.BlockSpec((1, width), lambda j, s: (0, 0)),
            pl.BlockSpec((1, width), lambda j, s: (0, 0)),
        ],
        out_specs=pl.BlockSpec((tm, width), lambda j, s: (_work_index(s, n_stage), j)),
        out_shape=jax.ShapeDtypeStruct((n, 2 * width), BF16),
        scratch_shapes=[pltpu.VMEM((d, width), BF16)],
        compiler_params=_params("arbitrary", "arbitrary"),
        name="gmlp_in",
    )(x, _row(g), w_all, _row(b), _row(ln_g), _row(ln_b))


def _gmlp_out_kernel(x_ref, u_ref, v_ref, ws_ref, bs_ref, w_ref, o_ref, wb_ref, gated_ref, *, n_stage):
    s = pl.program_id(0)

    @pl.when(s < n_stage)
    def _():
        _stage_rows(s, w_ref, wb_ref)

    @pl.when(s >= n_stage)
    def _():
        tm, width = u_ref.shape
        heads = ws_ref.shape[0]
        t_idx = lax.broadcasted_iota(jnp.int32, (GMLP_CHUNK, GMLP_CHUNK), 0)
        s_idx = lax.broadcasted_iota(jnp.int32, (GMLP_CHUNK, GMLP_CHUNK), 1)
        causal = t_idx >= s_idx
        for h in range(heads):
            w_h = jnp.where(causal, ws_ref[h], 0.0).astype(BF16)
            b_h = bs_ref[h]
            cols = slice(h * GMLP_HEAD_DIM, (h + 1) * GMLP_HEAD_DIM)
            for c in range(tm // GMLP_CHUNK):
                rows = slice(c * GMLP_CHUNK, (c + 1) * GMLP_CHUNK)
                sg = jnp.dot(w_h, v_ref[rows, cols], preferred_element_type=F32) + b_h
                gated_ref[rows, cols] = (u_ref[rows, cols].astype(F32) * sg).astype(BF16)
        o_ref[...] = x_ref[...] + jnp.dot(gated_ref[...], wb_ref[...], preferred_element_type=F32)


def _gmlp_out(x, z, w_s, b_s, w_all, layer):
    n, d = x.shape
    width = z.shape[1] // 2
    heads = w_s.shape[0]
    tm = _tile(n, 512)
    n_stage = width // STAGE_ROWS
    row = lambda c: (lambda s: (_work_index(s, n_stage), c))
    return pl.pallas_call(
        functools.partial(_gmlp_out_kernel, n_stage=n_stage),
        grid=(n_stage + n // tm,),
        in_specs=[
            pl.BlockSpec((tm, d), row(0)),
            pl.BlockSpec((tm, width), row(0)),
            pl.BlockSpec((tm, width), row(1)),
            _once((heads, GMLP_CHUNK, GMLP_CHUNK), lambda s: (0, 0, 0)),
            _once((heads, GMLP_CHUNK, 1), lambda s: (0, 0, 0)),
            _stage_spec(layer, width, d),
        ],
        out_specs=pl.BlockSpec((tm, d), row(0)),
        out_shape=jax.ShapeDtypeStruct((n, d), F32),
        scratch_shapes=[pltpu.VMEM((width, d), BF16), pltpu.VMEM((tm, width), BF16)],
        compiler_params=_params("arbitrary"),
        name="gmlp_out",
    )(x, z, z, w_s, b_s.reshape(heads, GMLP_CHUNK, 1), w_all)


def _mem_kv_kernel(x_ref, g_ref, w_ref, o_ref, h_ref):
    @pl.when(pl.program_id(0) == 0)
    def _():
        h_ref[...] = _rms(x_ref[...], g_ref[...]).astype(BF16)

    o_ref[...] = jnp.dot(h_ref[...], w_ref[...].astype(BF16), preferred_element_type=F32).astype(o_ref.dtype)


def _mem_kv(mem, g, w_all, layer):
    n, d = mem.shape
    m = w_all.shape[2]
    tn = _tile(m, 1024)
    return pl.pallas_call(
        _mem_kv_kernel,
        grid=(m // tn,),
        in_specs=[
            _once((n, d), lambda j: (0, 0)),
            _once((1, d), lambda j: (0, 0)),
            pl.BlockSpec((None, d, tn), lambda j: (layer, 0, j)),
        ],
        out_specs=pl.BlockSpec((n, tn), lambda j: (0, j)),
        out_shape=jax.ShapeDtypeStruct((n, m), BF16),
        scratch_shapes=[pltpu.VMEM((n, d), BF16)],
        compiler_params=_params("arbitrary"),
        name="mem_kv",
    )(mem, _row(g), w_all)


def _attn_kernel(x_ref, g_ref, wq_ref, kv_ref, wo_ref, o_ref, wqb_ref, wob_ref, heads_ref, *, n_stage):
    s = pl.program_id(0)

    @pl.when(s < n_stage)
    def _():
        _stage_rows(s, wq_ref, wqb_ref)
        _stage_rows(s, wo_ref, wob_ref)

    @pl.when(s >= n_stage)
    def _():
        x = x_ref[...]
        d = x.shape[1]
        hd = d // ATTN_HEADS
        q = jnp.dot(_rms(x, g_ref[...]).astype(BF16), wqb_ref[...], preferred_element_type=F32).astype(BF16)
        scale = hd ** -0.5
        for h in range(ATTN_HEADS):
            cols = slice(h * hd, (h + 1) * hd)
            k_h = kv_ref[:, h * hd:(h + 1) * hd]
            v_h = kv_ref[:, d + h * hd:d + (h + 1) * hd]
            sc = _nt_dot(q[:, cols], k_h) * scale
            p = jnp.exp(sc - jnp.max(sc, axis=-1, keepdims=True))
            p = p / jnp.sum(p, axis=-1, keepdims=True)
            heads_ref[:, cols] = jnp.dot(p.astype(BF16), v_h, preferred_element_type=F32).astype(BF16)
        o_ref[...] = x + jnp.dot(heads_ref[...], wob_ref[...], preferred_element_type=F32)


def _attention(x, g, wq_all, kv, wo_all, layer, seq):
    n, d = x.shape
    mem_len = kv.shape[0] // (n // seq)
    tm = _tile(seq, 512)
    per_batch = seq // tm
    n_stage = d // STAGE_ROWS
    row_map = lambda s: (_work_index(s, n_stage), 0)
    return pl.pallas_call(
        functools.partial(_attn_kernel, n_stage=n_stage),
        grid=(n_stage + n // tm,),
        in_specs=[
            pl.BlockSpec((tm, d), row_map),
            pl.BlockSpec((1, d), lambda s: (0, 0)),
            _stage_spec(layer, d, d),
            pl.BlockSpec((mem_len, 2 * d), lambda s: (_work_index(s, n_stage) // per_batch, 0)),
            _stage_spec(layer, d, d),
        ],
        out_specs=pl.BlockSpec((tm, d), row_map),
        out_shape=jax.ShapeDtypeStruct((n, d), F32),
        scratch_shapes=[pltpu.VMEM((d, d), BF16), pltpu.VMEM((d, d), BF16), pltpu.VMEM((tm, d), BF16)],
        compiler_params=_params("arbitrary"),
        name="attention",
    )(x, _row(g), wq_all, kv, wo_all)


def _mlp_kernel(x_ref, g_ref, gf_ref, wup_ref, wdn_ref, o_ref, h_ref, *, reorder, final_norm):
    j = pl.program_id(1)
    tm, d = h_ref.shape

    @pl.when(j == 0)
    def _():
        if reorder == "to_tmajor":
            for t in range(S5_CHUNK):
                o_ref[t] = x_ref[:, t, :]
        elif reorder == "from_tmajor":
            for t in range(S5_CHUNK):
                o_ref[:, t, :] = x_ref[t]
        else:
            o_ref[...] = x_ref[...]
        h_ref[...] = _rms(o_ref[...].reshape(tm, d), g_ref[...]).astype(BF16)

    a = jnp.dot(h_ref[...], wup_ref[...].astype(BF16), preferred_element_type=F32)
    a = jnp.square(jnp.maximum(a, 0.0)).astype(BF16)
    o_ref[...] += jnp.dot(a, wdn_ref[...].astype(BF16), preferred_element_type=F32).reshape(o_ref.shape)

    if final_norm:
        @pl.when(j == pl.num_programs(1) - 1)
        def _():
            o_ref[...] = _rms(o_ref[...], gf_ref[...])


def _mlp(x, seq, g, wup_all, wdn_all, layer, tmajor_in, tmajor_out, g_final=None):
    n, d = x.shape
    ff = wup_all.shape[2]
    tm = _tile(seq, 1024)
    tf = _tile(ff, 512)
    per_batch = seq // tm
    rb = tm // S5_CHUNK
    r = seq // S5_CHUNK
    nat = ((n // S5_CHUNK, S5_CHUNK, d), (rb, S5_CHUNK, d), lambda i, j: (i, 0, 0))
    tmj = ((n // seq, S5_CHUNK, r, d), (None, S5_CHUNK, rb, d), lambda i, j: (i // per_batch, 0, i % per_batch, 0))
    flat = ((n, d), (tm, d), lambda i, j: (i, 0))
    if tmajor_in == tmajor_out:
        reorder, src, dst = "none", flat, flat
    elif tmajor_out:
        reorder, src, dst = "to_tmajor", nat, tmj
    else:
        reorder, src, dst = "from_tmajor", tmj, nat
    gf = g if g_final is None else g_final
    out = pl.pallas_call(
        functools.partial(_mlp_kernel, reorder=reorder, final_norm=g_final is not None),
        grid=(n // tm, ff // tf),
        in_specs=[
            _once(src[1], src[2]),
            pl.BlockSpec((1, d), lambda i, j: (0, 0)),
            pl.BlockSpec((1, d), lambda i, j: (0, 0)),
            pl.BlockSpec((None, d, tf), lambda i, j: (layer, 0, j)),
            pl.BlockSpec((None, tf, d), lambda i, j: (layer, j, 0)),
        ],
        out_specs=pl.BlockSpec(dst[1], dst[2]),
        out_shape=jax.ShapeDtypeStruct(dst[0], F32),
        scratch_shapes=[pltpu.VMEM((tm, d), BF16)],
        compiler_params=_params("arbitrary", "arbitrary"),
        name="mlp",
    )(x.reshape(src[0]), _row(g), _row(gf), wup_all, wdn_all)
    return out.reshape(n, d)


def _s5_in_kernel(x_ref, g_ref, w_ref, o_ref, wt_ref, *, n_stage):
    s = pl.program_id(0)
    for c in range(n_stage):
        @pl.when(s == c)
        def _():
            wt_ref[:, c * STAGE_ROWS:(c + 1) * STAGE_ROWS] = w_ref[...].T.astype(BF16)

    @pl.when(s >= n_stage)
    def _():
        h = _rms(x_ref[...], g_ref[...]).astype(BF16)
        ut = _nt_dot(wt_ref[...], h)
        o_ref[...] = ut.astype(BF16).reshape(o_ref.shape)


def _s5_in(xt, g, w_all, layer):
    bsz, _, r, d = xt.shape
    width = w_all.shape[2]
    groups = width // S5_GROUP
    n_stage = d // STAGE_ROWS
    bt = lambda s: (_work_index(s, n_stage) // S5_CHUNK, _work_index(s, n_stage) % S5_CHUNK)
    return pl.pallas_call(
        functools.partial(_s5_in_kernel, n_stage=n_stage),
        grid=(n_stage + bsz * S5_CHUNK,),
        in_specs=[
            pl.BlockSpec((None, None, r, d), lambda s: bt(s) + (0, 0)),
            pl.BlockSpec((1, d), lambda s: (0, 0)),
            _stage_spec(layer, d, width),
        ],
        out_specs=pl.BlockSpec((None, groups, None, S5_GROUP, r), lambda s: (bt(s)[0], 0, bt(s)[1], 0, 0)),
        out_shape=jax.ShapeDtypeStruct((bsz, groups, S5_CHUNK, S5_GROUP, r), BF16),
        scratch_shapes=[pltpu.VMEM((width, d), BF16)],
        compiler_params=_params("arbitrary"),
        name="s5_in",
    )(xt, _row(g), w_all)


def _cmul(ar, ai, br, bi):
    return ar * br - ai * bi, ar * bi + ai * br


def _s5_prep_kernel(lr_ref, li_ref, ldt_ref, bre_ref, bim_ref, cre_ref, cim_ref, crp_ref, cip_ref, dsk_ref,
                    t_ref, wb_ref, wcr_ref, wci_ref, dre_ref, dim_ref):
    npair = lr_ref.shape[0]
    idx = lambda shape, dim: lax.broadcasted_iota(jnp.int32, shape, dim)
    rev_t = (idx((S5_PAIR, S5_BLOCK), 0) == S5_CHUNK - 1 - idx((S5_PAIR, S5_BLOCK), 1) // S5_GROUP).astype(BF16)
    tile_c = (idx((S5_GROUP, S5_BLOCK), 0) == idx((S5_GROUP, S5_BLOCK), 1) % S5_GROUP).astype(BF16)
    skip_mask = ((idx((S5_GROUP, S5_BLOCK), 1) // S5_GROUP == S5_CHUNK - 1)
                 & (idx((S5_GROUP, S5_BLOCK), 1) % S5_GROUP == idx((S5_GROUP, S5_BLOCK), 0)))
    k_col = idx((S5_POW_ROWS, 1), 0).astype(F32)
    pad_rows = jnp.zeros((S5_PAIR - S5_POW_ROWS, S5_PAIR), F32)

    def split_dot(x, m):
        hi = x.astype(BF16)
        lo = (x - hi.astype(F32)).astype(BF16)
        return jnp.dot(hi, m, preferred_element_type=F32) + jnp.dot(lo, m, preferred_element_type=F32)

    def rows_x16(tab):
        return jnp.concatenate([jnp.broadcast_to(tab[t + 1:t + 2], (S5_GROUP, S5_PAIR)) for t in range(S5_CHUNK)],
                               axis=0)

    for p in range(npair):
        lr = jnp.minimum(lr_ref[p], LAM_RE_MAX)
        li = li_ref[p]
        dt = jnp.exp(ldt_ref[p])
        zr, zi = lr * dt, li * dt
        mag = jnp.exp(k_col * zr)
        pw_r, pw_i = mag * jnp.cos(k_col * zi), mag * jnp.sin(k_col * zi)
        ar, ai = pw_r[1:2], pw_i[1:2]
        den = lr * lr + li * li
        qr = ((ar - 1.0) * lr + ai * li) / den
        qi = (ai * lr - (ar - 1.0) * li) / den
        dre_ref[p] = pw_r[S5_CHUNK:S5_CHUNK + 1]
        dim_ref[p] = pw_i[S5_CHUNK:S5_CHUNK + 1]
        nx_r, nx_i = rows_x16(pw_r), rows_x16(pw_i)
        sc_r, sc_i = _cmul(pw_r, pw_i, qr, qi)
        sp_r = split_dot(jnp.concatenate([sc_r, pad_rows], axis=0).T, rev_t)
        sp_i = split_dot(jnp.concatenate([sc_i, pad_rows], axis=0).T, rev_t)
        wb_r, wb_i = _cmul(sp_r, sp_i, split_dot(bre_ref[p], tile_c), split_dot(bim_ref[p], tile_c))
        for k in range(2):
            g = 2 * p + k
            rows = slice(k * S5_STATE, (k + 1) * S5_STATE)
            wb_g = jnp.concatenate([wb_r[rows], wb_i[rows]], axis=0).astype(BF16)
            wb_ref[g] = wb_g
            krev = split_dot(cre_ref[g], wb_g[:S5_STATE]) - split_dot(cim_ref[g], wb_g[S5_STATE:])
            krev = krev + jnp.where(skip_mask, dsk_ref[g], 0.0)
            kext = jnp.concatenate([krev, jnp.zeros_like(krev)], axis=1)
            for tp in range(S5_CHUNK):
                off = (S5_CHUNK - 1 - tp) * S5_GROUP
                win = kext if off == 0 else pltpu.roll(kext, 2 * S5_BLOCK - off, axis=1)
                t_ref[g, tp * S5_GROUP:(tp + 1) * S5_GROUP, :] = win[:, :S5_BLOCK].astype(BF16)
            c_r = jnp.concatenate([crp_ref[g]] * S5_CHUNK, axis=0)
            c_i = jnp.concatenate([cip_ref[g]] * S5_CHUNK, axis=0)
            wcr_ref[g] = (c_r * nx_r - c_i * nx_i).astype(BF16)
            wci_ref[g] = (-(c_r * nx_i + c_i * nx_r)).astype(BF16)


def _s5_prep(lam_re, lam_im, log_dt, bm_re, bm_im, cm_re, cm_im, d_skip):
    groups, state = lam_re.shape
    assert state == S5_STATE and groups % 2 == 0
    gb = _tile(groups, 8)
    pairs = lambda a: a.reshape(groups // 2, 1, S5_PAIR)
    half = jnp.eye(2, dtype=F32)[jnp.arange(groups) % 2]
    cpad = lambda a: (a[:, :, None, :] * half[:, None, :, None]).reshape(groups, S5_GROUP, S5_PAIR)
    g3 = lambda n, shape: pl.BlockSpec((n,) + shape, lambda i: (i, 0, 0))
    return pl.pallas_call(
        _s5_prep_kernel,
        grid=(groups // gb,),
        in_specs=[g3(gb // 2, (1, S5_PAIR)), g3(gb // 2, (1, S5_PAIR)), g3(gb // 2, (1, S5_PAIR)),
                  g3(gb // 2, (S5_PAIR, S5_GROUP)), g3(gb // 2, (S5_PAIR, S5_GROUP)),
                  g3(gb, (S5_GROUP, state)), g3(gb, (S5_GROUP, state)),
                  g3(gb, (S5_GROUP, S5_PAIR)), g3(gb, (S5_GROUP, S5_PAIR)), g3(gb, (1, S5_BLOCK))],
        out_specs=[g3(gb, (S5_BLOCK, S5_BLOCK)), g3(gb, (S5_PAIR, S5_BLOCK)), g3(gb, (S5_BLOCK, S5_PAIR)),
                   g3(gb, (S5_BLOCK, S5_PAIR)), g3(gb // 2, (1, S5_PAIR)), g3(gb // 2, (1, S5_PAIR))],
        out_shape=[jax.ShapeDtypeStruct((groups, S5_BLOCK, S5_BLOCK), BF16),
                   jax.ShapeDtypeStruct((groups, S5_PAIR, S5_BLOCK), BF16),
                   jax.ShapeDtypeStruct((groups, S5_BLOCK, S5_PAIR), BF16),
                   jax.ShapeDtypeStruct((groups, S5_BLOCK, S5_PAIR), BF16),
                   jax.ShapeDtypeStruct((groups // 2, 1, S5_PAIR), F32),
                   jax.ShapeDtypeStruct((groups // 2, 1, S5_PAIR), F32)],
        compiler_params=_params("arbitrary"),
        name="s5_prep",
    )(pairs(lam_re), pairs(lam_im), pairs(jnp.repeat(log_dt, state)),
      bm_re.reshape(groups // 2, S5_PAIR, S5_GROUP), bm_im.reshape(groups // 2, S5_PAIR, S5_GROUP),
      cm_re, cm_im, cpad(cm_re), cpad(cm_im), jnp.tile(d_skip.reshape(groups, 1, S5_GROUP), (1, 1, S5_CHUNK)))


def _s5_core_kernel(ut_ref, t_ref, wb_ref, wcr_ref, wci_ref, dre_ref, dim_ref, o_ref,
                    elr_ref, eli_ref, epr_ref, epi_ref):
    gb, _, r = ut_ref.shape
    for p in range(gb // 2):
        parts = [jnp.dot(wb_ref[2 * p + k], ut_ref[2 * p + k], preferred_element_type=F32) for k in range(2)]
        lanes = slice(p * S5_PAIR, (p + 1) * S5_PAIR)
        elr_ref[:, lanes] = jnp.concatenate([parts[0][:S5_STATE], parts[1][:S5_STATE]], axis=0).T
        eli_ref[:, lanes] = jnp.concatenate([parts[0][S5_STATE:], parts[1][S5_STATE:]], axis=0).T
    dre = jnp.concatenate([dre_ref[p] for p in range(gb // 2)], axis=1)
    dim = jnp.concatenate([dim_ref[p] for p in range(gb // 2)], axis=1)

    def step(i, carry):
        er, ei = carry
        epr_ref[pl.ds(i, 1), :] = er
        epi_ref[pl.ds(i, 1), :] = ei
        lr = elr_ref[pl.ds(i, 1), :]
        li = eli_ref[pl.ds(i, 1), :]
        return dre * er - dim * ei + lr, dre * ei + dim * er + li

    zero = jnp.zeros((1, gb * S5_STATE), F32)
    lax.fori_loop(0, r, step, (zero, zero))
    for g in range(gb):
        lanes = slice((g // 2) * S5_PAIR, (g // 2 + 1) * S5_PAIR)
        y = jnp.dot(t_ref[g], ut_ref[g], preferred_element_type=F32)
        y = y + _nt_dot(wcr_ref[g], epr_ref[:, lanes].astype(BF16))
        y = y + _nt_dot(wci_ref[g], epi_ref[:, lanes].astype(BF16))
        o_ref[g] = _gelu(y).astype(BF16)


def _s5_core(ut, tmat, wb, wcr, wci, dre, dim):
    bsz, groups, blk, r = ut.shape
    gb = _tile(groups, 8)
    g3 = lambda n, shape: pl.BlockSpec((n,) + shape, lambda b, i: (i, 0, 0))
    return pl.pallas_call(
        _s5_core_kernel,
        grid=(bsz, groups // gb),
        in_specs=[pl.BlockSpec((None, gb, blk, r), lambda b, i: (b, i, 0, 0)),
                  g3(gb, (blk, blk)), g3(gb, (S5_PAIR, blk)), g3(gb, (blk, S5_PAIR)), g3(gb, (blk, S5_PAIR)),
                  g3(gb // 2, (1, S5_PAIR)), g3(gb // 2, (1, S5_PAIR))],
        out_specs=pl.BlockSpec((None, gb, blk, r), lambda b, i: (b, i, 0, 0)),
        out_shape=jax.ShapeDtypeStruct(ut.shape, BF16),
        scratch_shapes=[pltpu.VMEM((r, gb * S5_STATE), F32) for _ in range(4)],
        compiler_params=_params("arbitrary", "arbitrary"),
        name="s5_core",
    )(ut, tmat, wb, wcr, wci, dre, dim)


def _s5_out_kernel(x_ref, y_ref, wv_ref, wg_ref, bv_ref, bg_ref, o_ref, wvb_ref, wgb_ref, *, n_stage):
    s = pl.program_id(1)

    @pl.when(s < n_stage)
    def _():
        _stage_rows(s, wv_ref, wvb_ref)
        _stage_rows(s, wg_ref, wgb_ref)

    @pl.when(s >= n_stage)
    def _():
        groups, gsz, r = y_ref.shape
        y = y_ref[...].reshape(groups * gsz, r).astype(F32).T.astype(BF16)
        val = jnp.dot(y, wvb_ref[...], preferred_element_type=F32) + bv_ref[...]
        gate = jnp.dot(y, wgb_ref[...], preferred_element_type=F32) + bg_ref[...]
        o_ref[...] = x_ref[...] + val * jax.nn.sigmoid(gate)


def _s5_out(xt, yt, w_all, layer, b_out):
    bsz, _, r, d = xt.shape
    width = w_all.shape[1]
    groups = width // S5_GROUP
    tn = _tile(d, 1024)
    nc = d // tn
    n_stage = width // STAGE_ROWS
    b2 = b_out.reshape(1, 2 * d)
    bt = lambda s: (_work_index(s, n_stage) // S5_CHUNK, _work_index(s, n_stage) % S5_CHUNK)
    return pl.pallas_call(
        functools.partial(_s5_out_kernel, n_stage=n_stage),
        grid=(nc, n_stage + bsz * S5_CHUNK),
        in_specs=[
            pl.BlockSpec((None, None, r, tn), lambda c, s: bt(s) + (0, c)),
            pl.BlockSpec((None, groups, None, S5_GROUP, r), lambda c, s: (bt(s)[0], 0, bt(s)[1], 0, 0)),
            pl.BlockSpec((None, STAGE_ROWS, tn), lambda c, s: (layer, jnp.minimum(s, n_stage - 1), c)),
            pl.BlockSpec((None, STAGE_ROWS, tn), lambda c, s: (layer, jnp.minimum(s, n_stage - 1), nc + c)),
            pl.BlockSpec((1, tn), lambda c, s: (0, c)),
            pl.BlockSpec((1, tn), lambda c, s: (0, nc + c)),
        ],
        out_specs=pl.BlockSpec((None, None, r, tn), lambda c, s: bt(s) + (0, c)),
        out_shape=jax.ShapeDtypeStruct(xt.shape, F32),
        scratch_shapes=[pltpu.VMEM((width, tn), BF16), pltpu.VMEM((width, tn), BF16)],
        compiler_params=_params("arbitrary", "arbitrary"),
        name="s5_out",
    )(xt, yt, w_all, w_all, b2, b2)


def _s5_mixer(x, seq, g, w_in_all, layer, lam_re, lam_im, log_dt, bm_re, bm_im, cm_re, cm_im, d_skip,
              w_out_all, b_out):
    n, d = x.shape
    bsz = n // seq
    r = seq // S5_CHUNK
    groups = w_in_all.shape[2] // S5_GROUP
    xt = x.reshape(bsz, S5_CHUNK, r, d)
    ut = _s5_in(xt, g, w_in_all, layer)
    tmat, wb, wcr, wci, dre, dim = _s5_prep(lam_re, lam_im, log_dt, bm_re, bm_im, cm_re, cm_im, d_skip)
    yt = _s5_core(ut.reshape(bsz, groups, S5_BLOCK, r), tmat, wb, wcr, wci, dre, dim)
    out = _s5_out(xt, yt.reshape(bsz, groups, S5_CHUNK, S5_GROUP, r), w_out_all, layer, b_out)
    return out.reshape(n, d)


def kernel(x, mem, g_mix, g_xattn, g_mem, g_ff, g_final, a_w_in, a_b_in, a_ln_g, a_ln_b, a_w_s, a_b_s, a_w_out, b_w_in, b_lam_re, b_lam_im, b_log_dt, b_bm_re, b_bm_im, b_cm_re, b_cm_im, b_d, b_w_out, b_b_out, x_w_q, x_w_kv, x_w_o, f_w_up, f_w_down):
    bsz, seq, d = x.shape
    depth = g_mix.shape[0]
    assert seq % GMLP_CHUNK == 0 and seq % S5_CHUNK == 0 and d % ATTN_HEADS == 0
    h = x.reshape(bsz * seq, d)
    mem2 = mem.reshape(-1, d)
    tmajor = False
    for i in range(depth):
        j = i // N_MIXERS
        if i % N_MIXERS == 0:
            assert not tmajor
            z = _gmlp_in(h, g_mix[i], a_w_in, j, a_b_in[j], a_ln_g[j], a_ln_b[j])
            h = _gmlp_out(h, z, a_w_s[j], a_b_s[j], a_w_out, j)
        else:
            assert tmajor
            h = _s5_mixer(h, seq, g_mix[i], b_w_in, j, b_lam_re[j], b_lam_im[j], b_log_dt[j], b_bm_re[j],
                          b_bm_im[j], b_cm_re[j], b_cm_im[j], b_d[j], b_w_out, b_b_out[j])
        kv = _mem_kv(mem2, g_mem[i], x_w_kv, i)
        h = _attention(h, g_xattn[i], x_w_q, kv, x_w_o, i, seq)
        last = i == depth - 1
        want_tmajor = (not last) and (i + 1) % N_MIXERS == 1
        h = _mlp(h, seq, g_ff[i], f_w_up, f_w_down, i, tmajor, want_tmajor, g_final if last else None)
        tmajor = want_tmajor
    return h.reshape(bsz, seq, d)
```

```python
import functools

import jax
import jax.numpy as jnp
from jax import lax
from jax.experimental import pallas as pl
from jax.experimental.pallas import tpu as pltpu

F32 = jnp.float32
BF16 = jnp.bfloat16

EPS = 1e-6
LAM_RE_MAX = -1e-4
N_MIXERS = 2

GMLP_CHUNK = 128
GMLP_HEAD_DIM = 128
ATTN_HEADS = 4
S5_GROUP = 16
S5_STATE = 64
S5_CHUNK = 16
S5_BLOCK = S5_CHUNK * S5_GROUP
S5_PAIR = 2 * S5_STATE
S5_POW_ROWS = 24
S5_SCAN_UNROLL = 4

STAGE_ROWS = 256
V7X_VMEM_LIMIT_BYTES = 60 * 1024 * 1024


def _params(*semantics):
    return pltpu.CompilerParams(dimension_semantics=semantics, vmem_limit_bytes=V7X_VMEM_LIMIT_BYTES)


def _once(block_shape, index_map):
    return pl.BlockSpec(block_shape, index_map, pipeline_mode=pl.Buffered(1))


def _tile(n, want):
    t = min(n, want)
    assert n % t == 0, (n, want)
    return t


def _rms(x, g):
    ms = jnp.mean(x * x, axis=-1, keepdims=True)
    return x * lax.rsqrt(ms + EPS) * g


def _gelu(x):
    return 0.5 * x * (1.0 + lax.erf(x * (2.0 ** -0.5)))


def _nt_dot(a, b):
    return lax.dot_general(a, b, (((1,), (1,)), ((), ())), preferred_element_type=F32)


def _row(a):
    return a.reshape(1, -1)


def _stage_spec(layer, k, ncols, col=0):
    last = k // STAGE_ROWS - 1
    return pl.BlockSpec((None, STAGE_ROWS, ncols), lambda *ids: (layer, jnp.minimum(ids[-1], last), col))


def _work_index(step, n_stage):
    return jnp.maximum(step - n_stage, 0)


def _stage_rows(step, src_ref, dst_ref):
    rows = pl.ds(pl.multiple_of(step * STAGE_ROWS, STAGE_ROWS), STAGE_ROWS)
    dst_ref[rows, :] = src_ref[...].astype(BF16)


def _gmlp_in_kernel(x_ref, g_ref, w_ref, b_ref, lng_ref, lnb_ref, o_ref, wb_ref, *, n_stage):
    s = pl.program_id(1)

    @pl.when(s < n_stage)
    def _():
        _stage_rows(s, w_ref, wb_ref)

    @pl.when(s >= n_stage)
    def _():
        h = _rms(x_ref[...], g_ref[...]).astype(BF16)
        z = _gelu(jnp.dot(h, wb_ref[...], preferred_element_type=F32) + b_ref[...])

        @pl.when(pl.program_id(0) == 0)
        def _():
            o_ref[...] = z.astype(o_ref.dtype)

        @pl.when(pl.program_id(0) == 1)
        def _():
            mu = jnp.mean(z, axis=-1, keepdims=True)
            zc = z - mu
            var = jnp.mean(zc * zc, axis=-1, keepdims=True)
            o_ref[...] = (zc * lax.rsqrt(var + EPS) * lng_ref[...] + lnb_ref[...]).astype(o_ref.dtype)


def _gmlp_in(x, g, w_all, layer, b, ln_g, ln_b):
    n, d = x.shape
    width = w_all.shape[2] // 2
    tm = _tile(n, 512)
    n_stage = d // STAGE_ROWS
    row_map = lambda j, s: (_work_index(s, n_stage), 0)
    return pl.pallas_call(
        functools.partial(_gmlp_in_kernel, n_stage=n_stage),
        grid=(2, n_stage + n // tm),
        in_specs=[
            pl.BlockSpec((tm, d), row_map),
            pl.BlockSpec((1, d), lambda j, s: (0, 0)),
            pl.BlockSpec((None, STAGE_ROWS, width), lambda j, s: (layer, jnp.minimum(s, n_stage - 1), j)),
            pl.BlockSpec((1, width), lambda j, s: (0, j)),
            pl.BlockSpec((1, width), lambda j, s: (0, 0)),
            pl.BlockSpec((1, width), lambda j, s: (0, 0)),
        ],
        out_specs=pl.BlockSpec((tm, width), lambda j, s: (_work_index(s, n_stage), j)),
        out_shape=jax.ShapeDtypeStruct((n, 2 * width), BF16),
        scratch_shapes=[pltpu.VMEM((d, width), BF16)],
        compiler_params=_params("arbitrary", "arbitrary"),
        name="gmlp_in",
    )(x, _row(g), w_all, _row(b), _row(ln_g), _row(ln_b))


def _gmlp_out_kernel(x_ref, u_ref, v_ref, ws_ref, bs_ref, w_ref, o_ref, wb_ref, gated_ref, *, n_stage):
    s = pl.program_id(0)

    @pl.when(s < n_stage)
    def _():
        _stage_rows(s, w_ref, wb_ref)

    @pl.when(s >= n_stage)
    def _():
        tm, width = u_ref.shape
        heads = ws_ref.shape[0]
        t_idx = lax.broadcasted_iota(jnp.int32, (GMLP_CHUNK, GMLP_CHUNK), 0)
        s_idx = lax.broadcasted_iota(jnp.int32, (GMLP_CHUNK, GMLP_CHUNK), 1)
        causal = t_idx >= s_idx
        for h in range(heads):
            w_h = jnp.where(causal, ws_ref[h], 0.0).astype(BF16)
            b_h = bs_ref[h]
            cols = slice(h * GMLP_HEAD_DIM, (h + 1) * GMLP_HEAD_DIM)
            for c in range(tm // GMLP_CHUNK):
                rows = slice(c * GMLP_CHUNK, (c + 1) * GMLP_CHUNK)
                sg = jnp.dot(w_h, v_ref[rows, cols], preferred_element_type=F32) + b_h
                gated_ref[rows, cols] = (u_ref[rows, cols].astype(F32) * sg).astype(BF16)
        o_ref[...] = x_ref[...] + jnp.dot(gated_ref[...], wb_ref[...], preferred_element_type=F32)


def _gmlp_out(x, z, w_s, b_s, w_all, layer):
    n, d = x.shape
    width = z.shape[1] // 2
    heads = w_s.shape[0]
    tm = _tile(n, 512)
    n_stage = width // STAGE_ROWS
    row = lambda c: (lambda s: (_work_index(s, n_stage), c))
    return pl.pallas_call(
        functools.partial(_gmlp_out_kernel, n_stage=n_stage),
        grid=(n_stage + n // tm,),
        in_specs=[
            pl.BlockSpec((tm, d), row(0)),
            pl.BlockSpec((tm, width), row(0)),
            pl.BlockSpec((tm, width), row(1)),
            _once((heads, GMLP_CHUNK, GMLP_CHUNK), lambda s: (0, 0, 0)),
            _once((heads, GMLP_CHUNK, 1), lambda s: (0, 0, 0)),
            _stage_spec(layer, width, d),
        ],
        out_specs=pl.BlockSpec((tm, d), row(0)),
        out_shape=jax.ShapeDtypeStruct((n, d), F32),
        scratch_shapes=[pltpu.VMEM((width, d), BF16), pltpu.VMEM((tm, width), BF16)],
        compiler_params=_params("arbitrary"),
        name="gmlp_out",
    )(x, z, z, w_s, b_s.reshape(heads, GMLP_CHUNK, 1), w_all)


def _mem_kv_kernel(x_ref, g_ref, w_ref, o_ref, h_ref):
    @pl.when(pl.program_id(0) == 0)
    def _():
        h_ref[...] = _rms(x_ref[...], g_ref[...]).astype(BF16)

    o_ref[...] = jnp.dot(h_ref[...], w_ref[...].astype(BF16), preferred_element_type=F32).astype(o_ref.dtype)


def _mem_kv(mem, g, w_all, layer):
    n, d = mem.shape
    m = w_all.shape[2]
    tn = _tile(m, 1024)
    return pl.pallas_call(
        _mem_kv_kernel,
        grid=(m // tn,),
        in_specs=[
            _once((n, d), lambda j: (0, 0)),
            _once((1, d), lambda j: (0, 0)),
            pl.BlockSpec((None, d, tn), lambda j: (layer, 0, j)),
        ],
        out_specs=pl.BlockSpec((n, tn), lambda j: (0, j)),
        out_shape=jax.ShapeDtypeStruct((n, m), BF16),
        scratch_shapes=[pltpu.VMEM((n, d), BF16)],
        compiler_params=_params("arbitrary"),
        name="mem_kv",
    )(mem, _row(g), w_all)


def _attn_kernel(x_ref, g_ref, wq_ref, kv_ref, wo_ref, o_ref, wqb_ref, wob_ref, heads_ref, *, n_stage):
    s = pl.program_id(0)

    @pl.when(s < n_stage)
    def _():
        _stage_rows(s, wq_ref, wqb_ref)
        _stage_rows(s, wo_ref, wob_ref)

    @pl.when(s >= n_stage)
    def _():
        x = x_ref[...]
        d = x.shape[1]
        hd = d // ATTN_HEADS
        q = jnp.dot(_rms(x, g_ref[...]).astype(BF16), wqb_ref[...], preferred_element_type=F32).astype(BF16)
        scale = hd ** -0.5
        for h in range(ATTN_HEADS):
            cols = slice(h * hd, (h + 1) * hd)
            k_h = kv_ref[:, h * hd:(h + 1) * hd]
            v_h = kv_ref[:, d + h * hd:d + (h + 1) * hd]
            sc = _nt_dot(q[:, cols], k_h) * scale
            p = jnp.exp(sc - jnp.max(sc, axis=-1, keepdims=True))
            p = p / jnp.sum(p, axis=-1, keepdims=True)
            heads_ref[:, cols] = jnp.dot(p.astype(BF16), v_h, preferred_element_type=F32).astype(BF16)
        o_ref[...] = x + jnp.dot(heads_ref[...], wob_ref[...], preferred_element_type=F32)


def _attention(x, g, wq_all, kv, wo_all, layer, seq):
    n, d = x.shape
    mem_len = kv.shape[0] // (n // seq)
    tm = _tile(seq, 512)
    per_batch = seq // tm
    n_stage = d // STAGE_ROWS
    row_map = lambda s: (_work_index(s, n_stage), 0)
    return pl.pallas_call(
        functools.partial(_attn_kernel, n_stage=n_stage),
        grid=(n_stage + n // tm,),
        in_specs=[
            pl.BlockSpec((tm, d), row_map),
            pl.BlockSpec((1, d), lambda s: (0, 0)),
            _stage_spec(layer, d, d),
            pl.BlockSpec((mem_len, 2 * d), lambda s: (_work_index(s, n_stage) // per_batch, 0)),
            _stage_spec(layer, d, d),
        ],
        out_specs=pl.BlockSpec((tm, d), row_map),
        out_shape=jax.ShapeDtypeStruct((n, d), F32),
        scratch_shapes=[pltpu.VMEM((d, d), BF16), pltpu.VMEM((d, d), BF16), pltpu.VMEM((tm, d), BF16)],
        compiler_params=_params("arbitrary"),
        name="attention",
    )(x, _row(g), wq_all, kv, wo_all)


def _mlp_kernel(x_hbm, g_ref, gf_ref, wup_ref, wdn_ref, o_ref, xbuf, h_ref, sem, *,
                reorder, final_norm, per_batch, n_tiles):
    i, j = pl.program_id(0), pl.program_id(1)
    tm, d = h_ref.shape
    rb = tm // S5_CHUNK

    def x_tile(tile):
        if reorder == "none":
            src = x_hbm.at[pl.ds(tile * tm, tm)]
        elif reorder == "to_tmajor":
            src = x_hbm.at[pl.ds(tile * rb, rb)]
        else:
            src = x_hbm.at[pl.ds((tile // per_batch) * S5_CHUNK, S5_CHUNK), pl.ds((tile % per_batch) * rb, rb)]
        return pltpu.make_async_copy(src, xbuf, sem)

    @pl.when(j == 0)
    def _():
        @pl.when(i == 0)
        def _():
            x_tile(0).start()

        x_tile(i).wait()
        if reorder == "to_tmajor":
            for t in range(S5_CHUNK):
                o_ref[t] = xbuf[:, t, :]
        elif reorder == "from_tmajor":
            for t in range(S5_CHUNK):
                o_ref[:, t, :] = xbuf[t]
        else:
            o_ref[...] = xbuf[...]
        h_ref[...] = _rms(o_ref[...].reshape(tm, d), g_ref[...]).astype(BF16)

        @pl.when(i + 1 < n_tiles)
        def _():
            x_tile(i + 1).start()

    a = jnp.dot(h_ref[...], wup_ref[...].astype(BF16), preferred_element_type=F32)
    a = jnp.square(jnp.maximum(a, 0.0)).astype(BF16)
    o_ref[...] += jnp.dot(a, wdn_ref[...].astype(BF16), preferred_element_type=F32).reshape(o_ref.shape)

    if final_norm:
        @pl.when(j == pl.num_programs(1) - 1)
        def _():
            o_ref[...] = _rms(o_ref[...], gf_ref[...])


def _mlp(x, seq, g, wup_all, wdn_all, layer, tmajor_in, tmajor_out, g_final=None):
    n, d = x.shape
    ff = wup_all.shape[2]
    tm = _tile(seq, 1024)
    tf = _tile(ff, 512)
    per_batch = seq // tm
    rb = tm // S5_CHUNK
    r = seq // S5_CHUNK
    nat = ((n // S5_CHUNK, S5_CHUNK, d), (rb, S5_CHUNK, d), lambda i, j: (i, 0, 0))
    tmj = ((n // seq * S5_CHUNK, r, d), (S5_CHUNK, rb, d), None)
    tmj_out = ((n // seq, S5_CHUNK, r, d), (None, S5_CHUNK, rb, d), lambda i, j: (i // per_batch, 0, i % per_batch, 0))
    flat = ((n, d), (tm, d), lambda i, j: (i, 0))
    if tmajor_in == tmajor_out:
        reorder, src, dst = "none", flat, flat
    elif tmajor_out:
        reorder, src, dst = "to_tmajor", nat, tmj_out
    else:
        reorder, src, dst = "from_tmajor", tmj, nat
    gf = g if g_final is None else g_final
    out = pl.pallas_call(
        functools.partial(_mlp_kernel, reorder=reorder, final_norm=g_final is not None,
                          per_batch=per_batch, n_tiles=n // tm),
        grid=(n // tm, ff // tf),
        in_specs=[
            pl.BlockSpec(memory_space=pl.ANY),
            pl.BlockSpec((1, d), lambda i, j: (0, 0)),
            pl.BlockSpec((1, d), lambda i, j: (0, 0)),
            pl.BlockSpec((None, d, tf), lambda i, j: (layer, 0, j)),
            pl.BlockSpec((None, tf, d), lambda i, j: (layer, j, 0)),
        ],
        out_specs=pl.BlockSpec(dst[1], dst[2]),
        out_shape=jax.ShapeDtypeStruct(dst[0], F32),
        scratch_shapes=[pltpu.VMEM(src[1], F32), pltpu.VMEM((tm, d), BF16), pltpu.SemaphoreType.DMA(())],
        compiler_params=_params("arbitrary", "arbitrary"),
        name="mlp",
    )(x.reshape(src[0]), _row(g), _row(gf), wup_all, wdn_all)
    return out.reshape(n, d)


def _s5_in_kernel(x_ref, g_ref, w_ref, o_ref, wt_ref, *, n_stage):
    s = pl.program_id(0)
    for c in range(n_stage):
        @pl.when(s == c)
        def _():
            wt_ref[:, c * STAGE_ROWS:(c + 1) * STAGE_ROWS] = w_ref[...].T.astype(BF16)

    @pl.when(s >= n_stage)
    def _():
        h = _rms(x_ref[...], g_ref[...]).astype(BF16)
        ut = _nt_dot(wt_ref[...], h)
        o_ref[...] = ut.astype(BF16).reshape(o_ref.shape)


def _s5_in(xt, g, w_all, layer):
    bsz, _, r, d = xt.shape
    width = w_all.shape[2]
    groups = width // S5_GROUP
    n_stage = d // STAGE_ROWS
    bt = lambda s: (_work_index(s, n_stage) // S5_CHUNK, _work_index(s, n_stage) % S5_CHUNK)
    return pl.pallas_call(
        functools.partial(_s5_in_kernel, n_stage=n_stage),
        grid=(n_stage + bsz * S5_CHUNK,),
        in_specs=[
            pl.BlockSpec((None, None, r, d), lambda s: bt(s) + (0, 0)),
            pl.BlockSpec((1, d), lambda s: (0, 0)),
            _stage_spec(layer, d, width),
        ],
        out_specs=pl.BlockSpec((None, groups, None, S5_GROUP, r), lambda s: (bt(s)[0], 0, bt(s)[1], 0, 0)),
        out_shape=jax.ShapeDtypeStruct((bsz, groups, S5_CHUNK, S5_GROUP, r), BF16),
        scratch_shapes=[pltpu.VMEM((width, d), BF16)],
        compiler_params=_params("arbitrary"),
        name="s5_in",
    )(xt, _row(g), w_all)


def _cmul(ar, ai, br, bi):
    return ar * br - ai * bi, ar * bi + ai * br


def _s5_prep_kernel(lr_ref, li_ref, ldt_ref, bre_ref, bim_ref, cre_ref, cim_ref, crp_ref, cip_ref, dsk_ref,
                    t_ref, wb_ref, wcr_ref, wci_ref, dre_ref, dim_ref):
    npair = lr_ref.shape[0]
    idx = lambda shape, dim: lax.broadcasted_iota(jnp.int32, shape, dim)
    rev_t = (idx((S5_PAIR, S5_BLOCK), 0) == S5_CHUNK - 1 - idx((S5_PAIR, S5_BLOCK), 1) // S5_GROUP).astype(BF16)
    tile_c = (idx((S5_GROUP, S5_BLOCK), 0) == idx((S5_GROUP, S5_BLOCK), 1) % S5_GROUP).astype(BF16)
    skip_mask = ((idx((S5_GROUP, S5_BLOCK), 1) // S5_GROUP == S5_CHUNK - 1)
                 & (idx((S5_GROUP, S5_BLOCK), 1) % S5_GROUP == idx((S5_GROUP, S5_BLOCK), 0)))
    k_col = idx((S5_POW_ROWS, 1), 0).astype(F32)
    pad_rows = jnp.zeros((S5_PAIR - S5_POW_ROWS, S5_PAIR), F32)

    def split_dot(x, m):
        hi = x.astype(BF16)
        lo = (x - hi.astype(F32)).astype(BF16)
        return jnp.dot(hi, m, preferred_element_type=F32) + jnp.dot(lo, m, preferred_element_type=F32)

    def rows_x16(tab):
        return jnp.concatenate([jnp.broadcast_to(tab[t + 1:t + 2], (S5_GROUP, S5_PAIR)) for t in range(S5_CHUNK)],
                               axis=0)

    for p in range(npair):
        lr = jnp.minimum(lr_ref[p], LAM_RE_MAX)
        li = li_ref[p]
        dt = jnp.exp(ldt_ref[p])
        zr, zi = lr * dt, li * dt
        mag = jnp.exp(k_col * zr)
        pw_r, pw_i = mag * jnp.cos(k_col * zi), mag * jnp.sin(k_col * zi)
        ar, ai = pw_r[1:2], pw_i[1:2]
        den = lr * lr + li * li
        qr = ((ar - 1.0) * lr + ai * li) / den
        qi = (ai * lr - (ar - 1.0) * li) / den
        dre_ref[p] = pw_r[S5_CHUNK:S5_CHUNK + 1]
        dim_ref[p] = pw_i[S5_CHUNK:S5_CHUNK + 1]
        nx_r, nx_i = rows_x16(pw_r), rows_x16(pw_i)
        sc_r, sc_i = _cmul(pw_r, pw_i, qr, qi)
        sp_r = split_dot(jnp.concatenate([sc_r, pad_rows], axis=0).T, rev_t)
        sp_i = split_dot(jnp.concatenate([sc_i, pad_rows], axis=0).T, rev_t)
        wb_r, wb_i = _cmul(sp_r, sp_i, split_dot(bre_ref[p], tile_c), split_dot(bim_ref[p], tile_c))
        for k in range(2):
            g = 2 * p + k
            rows = slice(k * S5_STATE, (k + 1) * S5_STATE)
            wb_g = jnp.concatenate([wb_r[rows], wb_i[rows]], axis=0).astype(BF16)
            wb_ref[g] = wb_g
            krev = split_dot(cre_ref[g], wb_g[:S5_STATE]) - split_dot(cim_ref[g], wb_g[S5_STATE:])
            krev = krev + jnp.where(skip_mask, dsk_ref[g], 0.0)
            kext = jnp.concatenate([krev, jnp.zeros_like(krev)], axis=1)
            for tp in range(S5_CHUNK):
                off = (S5_CHUNK - 1 - tp) * S5_GROUP
                win = kext if off == 0 else pltpu.roll(kext, 2 * S5_BLOCK - off, axis=1)
                t_ref[g, tp * S5_GROUP:(tp + 1) * S5_GROUP, :] = win[:, :S5_BLOCK].astype(BF16)
            c_r = jnp.concatenate([crp_ref[g]] * S5_CHUNK, axis=0)
            c_i = jnp.concatenate([cip_ref[g]] * S5_CHUNK, axis=0)
            wcr_ref[g] = (c_r * nx_r - c_i * nx_i).astype(BF16)
            wci_ref[g] = (-(c_r * nx_i + c_i * nx_r)).astype(BF16)


def _s5_prep(lam_re, lam_im, log_dt, bm_re, bm_im, cm_re, cm_im, d_skip):
    groups, state = lam_re.shape
    assert state == S5_STATE and groups % 2 == 0
    gb = _tile(groups, 8)
    pairs = lambda a: a.reshape(groups // 2, 1, S5_PAIR)
    half = jnp.eye(2, dtype=F32)[jnp.arange(groups) % 2]
    cpad = lambda a: (a[:, :, None, :] * half[:, None, :, None]).reshape(groups, S5_GROUP, S5_PAIR)
    g3 = lambda n, shape: pl.BlockSpec((n,) + shape, lambda i: (i, 0, 0))
    return pl.pallas_call(
        _s5_prep_kernel,
        grid=(groups // gb,),
        in_specs=[g3(gb // 2, (1, S5_PAIR)), g3(gb // 2, (1, S5_PAIR)), g3(gb // 2, (1, S5_PAIR)),
                  g3(gb // 2, (S5_PAIR, S5_GROUP)), g3(gb // 2, (S5_PAIR, S5_GROUP)),
                  g3(gb, (S5_GROUP, state)), g3(gb, (S5_GROUP, state)),
                  g3(gb, (S5_GROUP, S5_PAIR)), g3(gb, (S5_GROUP, S5_PAIR)), g3(gb, (1, S5_BLOCK))],
        out_specs=[g3(gb, (S5_BLOCK, S5_BLOCK)), g3(gb, (S5_PAIR, S5_BLOCK)), g3(gb, (S5_BLOCK, S5_PAIR)),
                   g3(gb, (S5_BLOCK, S5_PAIR)), g3(gb // 2, (1, S5_PAIR)), g3(gb // 2, (1, S5_PAIR))],
        out_shape=[jax.ShapeDtypeStruct((groups, S5_BLOCK, S5_BLOCK), BF16),
                   jax.ShapeDtypeStruct((groups, S5_PAIR, S5_BLOCK), BF16),
                   jax.ShapeDtypeStruct((groups, S5_BLOCK, S5_PAIR), BF16),
                   jax.ShapeDtypeStruct((groups, S5_BLOCK, S5_PAIR), BF16),
                   jax.ShapeDtypeStruct((groups // 2, 1, S5_PAIR), F32),
                   jax.ShapeDtypeStruct((groups // 2, 1, S5_PAIR), F32)],
        compiler_params=_params("arbitrary"),
        name="s5_prep",
    )(pairs(lam_re), pairs(lam_im), pairs(jnp.repeat(log_dt, state)),
      bm_re.reshape(groups // 2, S5_PAIR, S5_GROUP), bm_im.reshape(groups // 2, S5_PAIR, S5_GROUP),
      cm_re, cm_im, cpad(cm_re), cpad(cm_im), jnp.tile(d_skip.reshape(groups, 1, S5_GROUP), (1, 1, S5_CHUNK)))


def _s5_core_kernel(ut_ref, t_ref, wb_ref, wcr_ref, wci_ref, dre_ref, dim_ref, o_ref,
                    elr_ref, eli_ref, epr_ref, epi_ref):
    bsz, gb, _, r = ut_ref.shape
    for b in range(bsz):
        for p in range(gb // 2):
            parts = [jnp.dot(wb_ref[2 * p + k], ut_ref[b, 2 * p + k], preferred_element_type=F32) for k in range(2)]
            lanes = slice(p * S5_PAIR, (p + 1) * S5_PAIR)
            elr_ref[b, :, lanes] = jnp.concatenate([parts[0][:S5_STATE], parts[1][:S5_STATE]], axis=0).T
            eli_ref[b, :, lanes] = jnp.concatenate([parts[0][S5_STATE:], parts[1][S5_STATE:]], axis=0).T
    dre = jnp.concatenate([dre_ref[p] for p in range(gb // 2)], axis=1)
    dim = jnp.concatenate([dim_ref[p] for p in range(gb // 2)], axis=1)

    def step(i, carry):
        out = []
        for b in range(bsz):
            er, ei = carry[b]
            epr_ref[b, pl.ds(i, 1), :] = er
            epi_ref[b, pl.ds(i, 1), :] = ei
            lr = elr_ref[b, pl.ds(i, 1), :]
            li = eli_ref[b, pl.ds(i, 1), :]
            out.append((dre * er - dim * ei + lr, dre * ei + dim * er + li))
        return tuple(out)

    zero = jnp.zeros((1, gb * S5_STATE), F32)
    lax.fori_loop(0, r, step, tuple((zero, zero) for _ in range(bsz)), unroll=S5_SCAN_UNROLL)
    for b in range(bsz):
        for g in range(gb):
            lanes = slice((g // 2) * S5_PAIR, (g // 2 + 1) * S5_PAIR)
            y = jnp.dot(t_ref[g], ut_ref[b, g], preferred_element_type=F32)
            y = y + _nt_dot(wcr_ref[g], epr_ref[b, :, lanes].astype(BF16))
            y = y + _nt_dot(wci_ref[g], epi_ref[b, :, lanes].astype(BF16))
            o_ref[b, g] = _gelu(y).astype(BF16)


def _s5_core(ut, tmat, wb, wcr, wci, dre, dim):
    bsz, groups, blk, r = ut.shape
    gb = _tile(groups, 8)
    g3 = lambda n, shape: pl.BlockSpec((n,) + shape, lambda i: (i, 0, 0))
    return pl.pallas_call(
        _s5_core_kernel,
        grid=(groups // gb,),
        in_specs=[pl.BlockSpec((bsz, gb, blk, r), lambda i: (0, i, 0, 0)),
                  g3(gb, (blk, blk)), g3(gb, (S5_PAIR, blk)), g3(gb, (blk, S5_PAIR)), g3(gb, (blk, S5_PAIR)),
                  g3(gb // 2, (1, S5_PAIR)), g3(gb // 2, (1, S5_PAIR))],
        out_specs=pl.BlockSpec((bsz, gb, blk, r), lambda i: (0, i, 0, 0)),
        out_shape=jax.ShapeDtypeStruct(ut.shape, BF16),
        scratch_shapes=[pltpu.VMEM((bsz, r, gb * S5_STATE), F32) for _ in range(4)],
        compiler_params=_params("arbitrary"),
        name="s5_core",
    )(ut, tmat, wb, wcr, wci, dre, dim)


def _s5_out_kernel(x_ref, y_ref, wv_ref, wg_ref, bv_ref, bg_ref, o_ref, wvb_ref, wgb_ref, *, n_stage):
    s = pl.program_id(1)

    @pl.when(s < n_stage)
    def _():
        _stage_rows(s, wv_ref, wvb_ref)
        _stage_rows(s, wg_ref, wgb_ref)

    @pl.when(s >= n_stage)
    def _():
        groups, gsz, r = y_ref.shape
        y = y_ref[...].reshape(groups * gsz, r).astype(F32).T.astype(BF16)
        val = jnp.dot(y, wvb_ref[...], preferred_element_type=F32) + bv_ref[...]
        gate = jnp.dot(y, wgb_ref[...], preferred_element_type=F32) + bg_ref[...]
        o_ref[...] = x_ref[...] + val * jax.nn.sigmoid(gate)


def _s5_out(xt, yt, w_all, layer, b_out):
    bsz, _, r, d = xt.shape
    width = w_all.shape[1]
    groups = width // S5_GROUP
    tn = _tile(d, 2048)
    nc = d // tn
    n_stage = width // STAGE_ROWS
    b2 = b_out.reshape(1, 2 * d)
    bt = lambda s: (_work_index(s, n_stage) // S5_CHUNK, _work_index(s, n_stage) % S5_CHUNK)
    return pl.pallas_call(
        functools.partial(_s5_out_kernel, n_stage=n_stage),
        grid=(nc, n_stage + bsz * S5_CHUNK),
        in_specs=[
            pl.BlockSpec((None, None, r, tn), lambda c, s: bt(s) + (0, c)),
            pl.BlockSpec((None, groups, None, S5_GROUP, r), lambda c, s: (bt(s)[0], 0, bt(s)[1], 0, 0)),
            pl.BlockSpec((None, STAGE_ROWS, tn), lambda c, s: (layer, jnp.minimum(s, n_stage - 1), c)),
            pl.BlockSpec((None, STAGE_ROWS, tn), lambda c, s: (layer, jnp.minimum(s, n_stage - 1), nc + c)),
            pl.BlockSpec((1, tn), lambda c, s: (0, c)),
            pl.BlockSpec((1, tn), lambda c, s: (0, nc + c)),
        ],
        out_specs=pl.BlockSpec((None, None, r, tn), lambda c, s: bt(s) + (0, c)),
        out_shape=jax.ShapeDtypeStruct(xt.shape, F32),
        scratch_shapes=[pltpu.VMEM((width, tn), BF16), pltpu.VMEM((width, tn), BF16)],
        compiler_params=_params("arbitrary", "arbitrary"),
        name="s5_out",
    )(xt, yt, w_all, w_all, b2, b2)


def _s5_mixer(x, seq, g, w_in_all, layer, lam_re, lam_im, log_dt, bm_re, bm_im, cm_re, cm_im, d_skip,
              w_out_all, b_out):
    n, d = x.shape
    bsz = n // seq
    r = seq // S5_CHUNK
    groups = w_in_all.shape[2] // S5_GROUP
    xt = x.reshape(bsz, S5_CHUNK, r, d)
    ut = _s5_in(xt, g, w_in_all, layer)
    tmat, wb, wcr, wci, dre, dim = _s5_prep(lam_re, lam_im, log_dt, bm_re, bm_im, cm_re, cm_im, d_skip)
    yt = _s5_core(ut.reshape(bsz, groups, S5_BLOCK, r), tmat, wb, wcr, wci, dre, dim)
    out = _s5_out(xt, yt.reshape(bsz, groups, S5_CHUNK, S5_GROUP, r), w_out_all, layer, b_out)
    return out.reshape(n, d)


def kernel(x, mem, g_mix, g_xattn, g_mem, g_ff, g_final, a_w_in, a_b_in, a_ln_g, a_ln_b, a_w_s, a_b_s, a_w_out, b_w_in, b_lam_re, b_lam_im, b_log_dt, b_bm_re, b_bm_im, b_cm_re, b_cm_im, b_d, b_w_out, b_b_out, x_w_q, x_w_kv, x_w_o, f_w_up, f_w_down):
    bsz, seq, d = x.shape
    depth = g_mix.shape[0]
    assert seq % GMLP_CHUNK == 0 and seq % S5_CHUNK == 0 and d % ATTN_HEADS == 0
    h = x.reshape(bsz * seq, d)
    mem2 = mem.reshape(-1, d)
    tmajor = False
    for i in range(depth):
        j = i // N_MIXERS
        if i % N_MIXERS == 0:
            assert not tmajor
            z = _gmlp_in(h, g_mix[i], a_w_in, j, a_b_in[j], a_ln_g[j], a_ln_b[j])
            h = _gmlp_out(h, z, a_w_s[j], a_b_s[j], a_w_out, j)
        else:
            assert tmajor
            h = _s5_mixer(h, seq, g_mix[i], b_w_in, j, b_lam_re[j], b_lam_im[j], b_log_dt[j], b_bm_re[j],
                          b_bm_im[j], b_cm_re[j], b_cm_im[j], b_d[j], b_w_out, b_b_out[j])
        kv = _mem_kv(mem2, g_mem[i], x_w_kv, i)
        h = _attention(h, g_xattn[i], x_w_q, kv, x_w_o, i, seq)
        last = i == depth - 1
        want_tmajor = (not last) and (i + 1) % N_MIXERS == 1
        h = _mlp(h, seq, g_ff[i], f_w_up, f_w_down, i, tmajor, want_tmajor, g_final if last else None)
        tmajor = want_tmajor
    return h.reshape(bsz, seq, d)
```

```python
import functools

import jax
import jax.numpy as jnp
from jax import lax
from jax.experimental import pallas as pl
from jax.experimental.pallas import tpu as pltpu

F32 = jnp.float32
BF16 = jnp.bfloat16

EPS = 1e-6
LAM_RE_MAX = -1e-4
N_MIXERS = 2

LANES = 128
GMLP_CHUNK = 128
GMLP_HEAD_DIM = 128
ATTN_HEADS = 4
S5_GROUP = 16
S5_STATE = 64
S5_CHUNK = 16
S5_BLOCK = S5_CHUNK * S5_GROUP
S5_PAIR = 2 * S5_STATE
S5_POW_ROWS = 24
S5_SCAN_UNROLL = 4

STAGE_ROWS = 256
V7X_VMEM_LIMIT_BYTES = 60 * 1024 * 1024


def _params(*semantics):
    return pltpu.CompilerParams(dimension_semantics=semantics, vmem_limit_bytes=V7X_VMEM_LIMIT_BYTES)


def _once(block_shape, index_map):
    return pl.BlockSpec(block_shape, index_map, pipeline_mode=pl.Buffered(1))


def _tile(n, want):
    t = min(n, want)
    assert n % t == 0, (n, want)
    return t


def _rms(x, g):
    ms = jnp.mean(x * x, axis=-1, keepdims=True)
    return x * lax.rsqrt(ms + EPS) * g


def _gelu(x):
    return 0.5 * x * (1.0 + lax.erf(x * (2.0 ** -0.5)))


def _nt_dot(a, b):
    return lax.dot_general(a, b, (((1,), (1,)), ((), ())), preferred_element_type=F32)


def _row(a):
    return a.reshape(1, -1)


def _work_index(step, n_stage):
    return jnp.maximum(step - n_stage, 0)


def _stage_rows(step, src_ref, dst_ref):
    rows = pl.ds(pl.multiple_of(step * STAGE_ROWS, STAGE_ROWS), STAGE_ROWS)
    dst_ref[rows, :] = src_ref[...].astype(BF16)


def _cast_plan(jobs, n_steps, step_of):
    arrays, in_specs, out_specs, out_shapes = [], [], [], []
    for w_all, layer, transpose in jobs:
        _, k, n = w_all.shape
        n_blocks = min(n_steps, k // LANES) if transpose else n_steps
        assert n_steps % n_blocks == 0 and k % n_blocks == 0, (k, n_steps)
        rows, per = k // n_blocks, n_steps // n_blocks
        blk = lambda *ids, per=per: step_of(*ids) // per
        arrays.append(w_all)
        in_specs.append(pl.BlockSpec((None, rows, n), lambda *ids, layer=layer, blk=blk: (layer, blk(*ids), 0)))
        if transpose:
            out_specs.append(pl.BlockSpec((n, rows), lambda *ids, blk=blk: (0, blk(*ids))))
            out_shapes.append(jax.ShapeDtypeStruct((n, k), BF16))
        else:
            out_specs.append(pl.BlockSpec((rows, n), lambda *ids, blk=blk: (blk(*ids), 0)))
            out_shapes.append(jax.ShapeDtypeStruct((k, n), BF16))
    return arrays, in_specs, out_specs, out_shapes


def _cast_blocks(src_refs, dst_refs, jobs):
    for src, dst, (_, _, transpose) in zip(src_refs, dst_refs, jobs):
        v = src[...]
        dst[...] = (v.T if transpose else v).astype(BF16)


def _gmlp_in_kernel(*refs, n_stage, jobs):
    nj = len(jobs)
    x_ref, g_ref, w_ref, b_ref, lng_ref, lnb_ref = refs[:6]
    cast_src = refs[6:6 + nj]
    o_ref = refs[6 + nj]
    cast_dst = refs[7 + nj:7 + 2 * nj]
    wb_ref = refs[7 + 2 * nj]
    s = pl.program_id(1)

    @pl.when(s < n_stage)
    def _():
        _stage_rows(s, w_ref, wb_ref)

    @pl.when(s >= n_stage)
    def _():
        _cast_blocks(cast_src, cast_dst, jobs)
        h = _rms(x_ref[...], g_ref[...]).astype(BF16)
        z = _gelu(jnp.dot(h, wb_ref[...], preferred_element_type=F32) + b_ref[...])

        @pl.when(pl.program_id(0) == 0)
        def _():
            o_ref[...] = z.astype(o_ref.dtype)

        @pl.when(pl.program_id(0) == 1)
        def _():
            mu = jnp.mean(z, axis=-1, keepdims=True)
            zc = z - mu
            var = jnp.mean(zc * zc, axis=-1, keepdims=True)
            o_ref[...] = (zc * lax.rsqrt(var + EPS) * lng_ref[...] + lnb_ref[...]).astype(o_ref.dtype)


def _gmlp_in(x, g, w_all, layer, b, ln_g, ln_b, jobs):
    n, d = x.shape
    width = w_all.shape[2] // 2
    tm = _tile(n, 512)
    n_tiles = n // tm
    n_stage = d // STAGE_ROWS
    work = lambda s: _work_index(s, n_stage)
    cast_arrays, cast_in, cast_out, cast_shapes = _cast_plan(jobs, 2 * n_tiles, lambda j, s: j * n_tiles + work(s))
    outs = pl.pallas_call(
        functools.partial(_gmlp_in_kernel, n_stage=n_stage, jobs=jobs),
        grid=(2, n_stage + n_tiles),
        in_specs=[
            pl.BlockSpec((tm, d), lambda j, s: (work(s), 0)),
            pl.BlockSpec((1, d), lambda j, s: (0, 0)),
            pl.BlockSpec((None, STAGE_ROWS, width), lambda j, s: (layer, jnp.minimum(s, n_stage - 1), j)),
            pl.BlockSpec((1, width), lambda j, s: (0, j)),
            pl.BlockSpec((1, width), lambda j, s: (0, 0)),
            pl.BlockSpec((1, width), lambda j, s: (0, 0)),
        ] + cast_in,
        out_specs=[pl.BlockSpec((tm, width), lambda j, s: (work(s), j))] + cast_out,
        out_shape=[jax.ShapeDtypeStruct((n, 2 * width), BF16)] + cast_shapes,
        scratch_shapes=[pltpu.VMEM((d, width), BF16)],
        compiler_params=_params("arbitrary", "arbitrary"),
        name="gmlp_in",
    )(x, _row(g), w_all, _row(b), _row(ln_g), _row(ln_b), *cast_arrays)
    return outs[0], outs[1:]


def _gmlp_out_kernel(x_ref, u_ref, v_ref, ws_ref, bs_ref, w_ref, o_ref, gated_ref):
    tm, width = u_ref.shape
    heads = ws_ref.shape[0]
    t_idx = lax.broadcasted_iota(jnp.int32, (GMLP_CHUNK, GMLP_CHUNK), 0)
    s_idx = lax.broadcasted_iota(jnp.int32, (GMLP_CHUNK, GMLP_CHUNK), 1)
    causal = t_idx >= s_idx
    for h in range(heads):
        w_h = jnp.where(causal, ws_ref[h], 0.0).astype(BF16)
        b_h = bs_ref[h]
        cols = slice(h * GMLP_HEAD_DIM, (h + 1) * GMLP_HEAD_DIM)
        for c in range(tm // GMLP_CHUNK):
            rows = slice(c * GMLP_CHUNK, (c + 1) * GMLP_CHUNK)
            sg = jnp.dot(w_h, v_ref[rows, cols], preferred_element_type=F32) + b_h
            gated_ref[rows, cols] = (u_ref[rows, cols].astype(F32) * sg).astype(BF16)
    o_ref[...] = x_ref[...] + jnp.dot(gated_ref[...], w_ref[...], preferred_element_type=F32)


def _gmlp_out(x, z, w_s, b_s, w_out):
    n, d = x.shape
    width = z.shape[1] // 2
    heads = w_s.shape[0]
    tm = _tile(n, 512)
    return pl.pallas_call(
        _gmlp_out_kernel,
        grid=(n // tm,),
        in_specs=[
            pl.BlockSpec((tm, d), lambda i: (i, 0)),
            pl.BlockSpec((tm, width), lambda i: (i, 0)),
            pl.BlockSpec((tm, width), lambda i: (i, 1)),
            _once((heads, GMLP_CHUNK, GMLP_CHUNK), lambda i: (0, 0, 0)),
            _once((heads, GMLP_CHUNK, 1), lambda i: (0, 0, 0)),
            _once((width, d), lambda i: (0, 0)),
        ],
        out_specs=pl.BlockSpec((tm, d), lambda i: (i, 0)),
        out_shape=jax.ShapeDtypeStruct((n, d), F32),
        scratch_shapes=[pltpu.VMEM((tm, width), BF16)],
        compiler_params=_params("arbitrary"),
        name="gmlp_out",
    )(x, z, z, w_s, b_s.reshape(heads, GMLP_CHUNK, 1), w_out)


def _mem_kv_kernel(x_ref, g_ref, w_ref, o_ref, h_ref):
    @pl.when(pl.program_id(0) == 0)
    def _():
        h_ref[...] = _rms(x_ref[...], g_ref[...]).astype(BF16)

    o_ref[...] = jnp.dot(h_ref[...], w_ref[...], preferred_element_type=F32).astype(o_ref.dtype)


def _mem_kv(mem, g, w_kv):
    n, d = mem.shape
    m = w_kv.shape[1]
    tn = _tile(m, 2048)
    return pl.pallas_call(
        _mem_kv_kernel,
        grid=(m // tn,),
        in_specs=[
            _once((n, d), lambda j: (0, 0)),
            _once((1, d), lambda j: (0, 0)),
            pl.BlockSpec((d, tn), lambda j: (0, j)),
        ],
        out_specs=pl.BlockSpec((n, tn), lambda j: (0, j)),
        out_shape=jax.ShapeDtypeStruct((n, m), BF16),
        scratch_shapes=[pltpu.VMEM((n, d), BF16)],
        compiler_params=_params("arbitrary"),
        name="mem_kv",
    )(mem, _row(g), w_kv)


def _attn_kernel(x_ref, g_ref, wq_ref, kv_ref, wo_ref, o_ref, heads_ref):
    x = x_ref[...]
    d = x.shape[1]
    hd = d // ATTN_HEADS
    q = jnp.dot(_rms(x, g_ref[...]).astype(BF16), wq_ref[...], preferred_element_type=F32).astype(BF16)
    scale = hd ** -0.5
    for h in range(ATTN_HEADS):
        cols = slice(h * hd, (h + 1) * hd)
        k_h = kv_ref[:, h * hd:(h + 1) * hd]
        v_h = kv_ref[:, d + h * hd:d + (h + 1) * hd]
        sc = _nt_dot(q[:, cols], k_h) * scale
        p = jnp.exp(sc - jnp.max(sc, axis=-1, keepdims=True))
        p = p / jnp.sum(p, axis=-1, keepdims=True)
        heads_ref[:, cols] = jnp.dot(p.astype(BF16), v_h, preferred_element_type=F32).astype(BF16)
    o_ref[...] = x + jnp.dot(heads_ref[...], wo_ref[...], preferred_element_type=F32)


def _attention(x, g, w_q, kv, w_o, seq):
    n, d = x.shape
    mem_len = kv.shape[0] // (n // seq)
    tm = _tile(seq, 512)
    per_batch = seq // tm
    return pl.pallas_call(
        _attn_kernel,
        grid=(n // tm,),
        in_specs=[
            pl.BlockSpec((tm, d), lambda i: (i, 0)),
            pl.BlockSpec((1, d), lambda i: (0, 0)),
            _once((d, d), lambda i: (0, 0)),
            pl.BlockSpec((mem_len, 2 * d), lambda i: (i // per_batch, 0)),
            _once((d, d), lambda i: (0, 0)),
        ],
        out_specs=pl.BlockSpec((tm, d), lambda i: (i, 0)),
        out_shape=jax.ShapeDtypeStruct((n, d), F32),
        scratch_shapes=[pltpu.VMEM((tm, d), BF16)],
        compiler_params=_params("arbitrary"),
        name="attention",
    )(x, _row(g), w_q, kv, w_o)


def _mlp_kernel(*refs, reorder, final_norm, per_batch, n_tiles, jobs):
    nj = len(jobs)
    x_hbm, g_ref, gf_ref, wup_ref, wdn_ref = refs[:5]
    cast_src = refs[5:5 + nj]
    o_ref = refs[5 + nj]
    cast_dst = refs[6 + nj:6 + 2 * nj]
    xbuf, h_ref, sem = refs[6 + 2 * nj:]
    i, j = pl.program_id(0), pl.program_id(1)
    tm, d = h_ref.shape
    rb = tm // S5_CHUNK

    def x_tile(tile):
        if reorder == "none":
            src = x_hbm.at[pl.ds(tile * tm, tm)]
        elif reorder == "to_tmajor":
            src = x_hbm.at[pl.ds(tile * rb, rb)]
        else:
            src = x_hbm.at[pl.ds((tile // per_batch) * S5_CHUNK, S5_CHUNK), pl.ds((tile % per_batch) * rb, rb)]
        return pltpu.make_async_copy(src, xbuf, sem)

    @pl.when(j == 0)
    def _():
        @pl.when(i == 0)
        def _():
            x_tile(0).start()

        x_tile(i).wait()
        if reorder == "to_tmajor":
            for t in range(S5_CHUNK):
                o_ref[t] = xbuf[:, t, :]
        elif reorder == "from_tmajor":
            for t in range(S5_CHUNK):
                o_ref[:, t, :] = xbuf[t]
        else:
            o_ref[...] = xbuf[...]
        h_ref[...] = _rms(o_ref[...].reshape(tm, d), g_ref[...]).astype(BF16)

        @pl.when(i + 1 < n_tiles)
        def _():
            x_tile(i + 1).start()

    _cast_blocks(cast_src, cast_dst, jobs)
    a = jnp.dot(h_ref[...], wup_ref[...].astype(BF16), preferred_element_type=F32)
    a = jnp.square(jnp.maximum(a, 0.0)).astype(BF16)
    o_ref[...] += jnp.dot(a, wdn_ref[...].astype(BF16), preferred_element_type=F32).reshape(o_ref.shape)

    if final_norm:
        @pl.when(j == pl.num_programs(1) - 1)
        def _():
            o_ref[...] = _rms(o_ref[...], gf_ref[...])


def _mlp(x, seq, g, wup_all, wdn_all, layer, tmajor_in, tmajor_out, g_final=None, jobs=()):
    n, d = x.shape
    ff = wup_all.shape[2]
    tm = _tile(seq, 1024)
    tf = _tile(ff, 512)
    per_batch = seq // tm
    n_tiles, n_ff = n // tm, ff // tf
    rb = tm // S5_CHUNK
    r = seq // S5_CHUNK
    nat = ((n // S5_CHUNK, S5_CHUNK, d), (rb, S5_CHUNK, d), lambda i, j: (i, 0, 0))
    tmj = ((n // seq * S5_CHUNK, r, d), (S5_CHUNK, rb, d), None)
    tmj_out = ((n // seq, S5_CHUNK, r, d), (None, S5_CHUNK, rb, d), lambda i, j: (i // per_batch, 0, i % per_batch, 0))
    flat = ((n, d), (tm, d), lambda i, j: (i, 0))
    if tmajor_in == tmajor_out:
        reorder, src, dst = "none", flat, flat
    elif tmajor_out:
        reorder, src, dst = "to_tmajor", nat, tmj_out
    else:
        reorder, src, dst = "from_tmajor", tmj, nat
    gf = g if g_final is None else g_final
    cast_arrays, cast_in, cast_out, cast_shapes = _cast_plan(jobs, n_tiles * n_ff, lambda i, j: i * n_ff + j)
    outs = pl.pallas_call(
        functools.partial(_mlp_kernel, reorder=reorder, final_norm=g_final is not None,
                          per_batch=per_batch, n_tiles=n_tiles, jobs=jobs),
        grid=(n_tiles, n_ff),
        in_specs=[
            pl.BlockSpec(memory_space=pl.ANY),
            pl.BlockSpec((1, d), lambda i, j: (0, 0)),
            pl.BlockSpec((1, d), lambda i, j: (0, 0)),
            pl.BlockSpec((None, d, tf), lambda i, j: (layer, 0, j)),
            pl.BlockSpec((None, tf, d), lambda i, j: (layer, j, 0)),
        ] + cast_in,
        out_specs=[pl.BlockSpec(dst[1], dst[2])] + cast_out,
        out_shape=[jax.ShapeDtypeStruct(dst[0], F32)] + cast_shapes,
        scratch_shapes=[pltpu.VMEM(src[1], F32), pltpu.VMEM((tm, d), BF16), pltpu.SemaphoreType.DMA(())],
        compiler_params=_params("arbitrary", "arbitrary"),
        name="mlp",
    )(x.reshape(src[0]), _row(g), _row(gf), wup_all, wdn_all, *cast_arrays)
    return outs[0].reshape(n, d), outs[1:]


def _s5_in_kernel(x_ref, g_ref, wt_ref, o_ref):
    h = _rms(x_ref[...], g_ref[...]).astype(BF16)
    ut = _nt_dot(wt_ref[...], h)
    o_ref[...] = ut.astype(BF16).reshape(o_ref.shape)


def _s5_in(xt, g, w_in_t):
    bsz, _, r, d = xt.shape
    width = w_in_t.shape[0]
    groups = width // S5_GROUP
    return pl.pallas_call(
        _s5_in_kernel,
        grid=(bsz, S5_CHUNK),
        in_specs=[
            pl.BlockSpec((None, None, r, d), lambda b, t: (b, t, 0, 0)),
            pl.BlockSpec((1, d), lambda b, t: (0, 0)),
            _once((width, d), lambda b, t: (0, 0)),
        ],
        out_specs=pl.BlockSpec((None, groups, None, S5_GROUP, r), lambda b, t: (b, 0, t, 0, 0)),
        out_shape=jax.ShapeDtypeStruct((bsz, groups, S5_CHUNK, S5_GROUP, r), BF16),
        compiler_params=_params("arbitrary", "arbitrary"),
        name="s5_in",
    )(xt, _row(g), w_in_t)


def _cmul(ar, ai, br, bi):
    return ar * br - ai * bi, ar * bi + ai * br


def _s5_prep_kernel(lr_ref, li_ref, ldt_ref, bre_ref, bim_ref, cre_ref, cim_ref, crp_ref, cip_ref, dsk_ref,
                    t_ref, wb_ref, wcr_ref, wci_ref, dre_ref, dim_ref):
    npair = lr_ref.shape[0]
    idx = lambda shape, dim: lax.broadcasted_iota(jnp.int32, shape, dim)
    rev_t = (idx((S5_PAIR, S5_BLOCK), 0) == S5_CHUNK - 1 - idx((S5_PAIR, S5_BLOCK), 1) // S5_GROUP).astype(BF16)
    tile_c = (idx((S5_GROUP, S5_BLOCK), 0) == idx((S5_GROUP, S5_BLOCK), 1) % S5_GROUP).astype(BF16)
    skip_mask = ((idx((S5_GROUP, S5_BLOCK), 1) // S5_GROUP == S5_CHUNK - 1)
                 & (idx((S5_GROUP, S5_BLOCK), 1) % S5_GROUP == idx((S5_GROUP, S5_BLOCK), 0)))
    k_col = idx((S5_POW_ROWS, 1), 0).astype(F32)
    pad_rows = jnp.zeros((S5_PAIR - S5_POW_ROWS, S5_PAIR), F32)

    def split_dot(x, m):
        hi = x.astype(BF16)
        lo = (x - hi.astype(F32)).astype(BF16)
        return jnp.dot(hi, m, preferred_element_type=F32) + jnp.dot(lo, m, preferred_element_type=F32)

    def rows_x16(tab):
        return jnp.concatenate([jnp.broadcast_to(tab[t + 1:t + 2], (S5_GROUP, S5_PAIR)) for t in range(S5_CHUNK)],
                               axis=0)

    for p in range(npair):
        lr = jnp.minimum(lr_ref[p], LAM_RE_MAX)
        li = li_ref[p]
        dt = jnp.exp(ldt_ref[p])
        zr, zi = lr * dt, li * dt
        mag = jnp.exp(k_col * zr)
        pw_r, pw_i = mag * jnp.cos(k_col * zi), mag * jnp.sin(k_col * zi)
        ar, ai = pw_r[1:2], pw_i[1:2]
        den = lr * lr + li * li
        qr = ((ar - 1.0) * lr + ai * li) / den
        qi = (ai * lr - (ar - 1.0) * li) / den
        dre_ref[p] = pw_r[S5_CHUNK:S5_CHUNK + 1]
        dim_ref[p] = pw_i[S5_CHUNK:S5_CHUNK + 1]
        nx_r, nx_i = rows_x16(pw_r), rows_x16(pw_i)
        sc_r, sc_i = _cmul(pw_r, pw_i, qr, qi)
        sp_r = split_dot(jnp.concatenate([sc_r, pad_rows], axis=0).T, rev_t)
        sp_i = split_dot(jnp.concatenate([sc_i, pad_rows], axis=0).T, rev_t)
        wb_r, wb_i = _cmul(sp_r, sp_i, split_dot(bre_ref[p], tile_c), split_dot(bim_ref[p], tile_c))
        for k in range(2):
            g = 2 * p + k
            rows = slice(k * S5_STATE, (k + 1) * S5_STATE)
            wb_g = jnp.concatenate([wb_r[rows], wb_i[rows]], axis=0).astype(BF16)
            wb_ref[g] = wb_g
            krev = split_dot(cre_ref[g], wb_g[:S5_STATE]) - split_dot(cim_ref[g], wb_g[S5_STATE:])
            krev = krev + jnp.where(skip_mask, dsk_ref[g], 0.0)
            kext = jnp.concatenate([krev, jnp.zeros_like(krev)], axis=1)
            for tp in range(S5_CHUNK):
                off = (S5_CHUNK - 1 - tp) * S5_GROUP
                win = kext if off == 0 else pltpu.roll(kext, 2 * S5_BLOCK - off, axis=1)
                t_ref[g, tp * S5_GROUP:(tp + 1) * S5_GROUP, :] = win[:, :S5_BLOCK].astype(BF16)
            c_r = jnp.concatenate([crp_ref[g]] * S5_CHUNK, axis=0)
            c_i = jnp.concatenate([cip_ref[g]] * S5_CHUNK, axis=0)
            wcr_ref[g] = (c_r * nx_r - c_i * nx_i).astype(BF16)
            wci_ref[g] = (-(c_r * nx_i + c_i * nx_r)).astype(BF16)


def _s5_prep(lam_re, lam_im, log_dt, bm_re, bm_im, cm_re, cm_im, d_skip):
    groups, state = lam_re.shape
    assert state == S5_STATE and groups % 2 == 0
    gb = _tile(groups, 8)
    pairs = lambda a: a.reshape(groups // 2, 1, S5_PAIR)
    half = jnp.eye(2, dtype=F32)[jnp.arange(groups) % 2]
    cpad = lambda a: (a[:, :, None, :] * half[:, None, :, None]).reshape(groups, S5_GROUP, S5_PAIR)
    g3 = lambda n, shape: pl.BlockSpec((n,) + shape, lambda i: (i, 0, 0))
    return pl.pallas_call(
        _s5_prep_kernel,
        grid=(groups // gb,),
        in_specs=[g3(gb // 2, (1, S5_PAIR)), g3(gb // 2, (1, S5_PAIR)), g3(gb // 2, (1, S5_PAIR)),
                  g3(gb // 2, (S5_PAIR, S5_GROUP)), g3(gb // 2, (S5_PAIR, S5_GROUP)),
                  g3(gb, (S5_GROUP, state)), g3(gb, (S5_GROUP, state)),
                  g3(gb, (S5_GROUP, S5_PAIR)), g3(gb, (S5_GROUP, S5_PAIR)), g3(gb, (1, S5_BLOCK))],
        out_specs=[g3(gb, (S5_BLOCK, S5_BLOCK)), g3(gb, (S5_PAIR, S5_BLOCK)), g3(gb, (S5_BLOCK, S5_PAIR)),
                   g3(gb, (S5_BLOCK, S5_PAIR)), g3(gb // 2, (1, S5_PAIR)), g3(gb // 2, (1, S5_PAIR))],
        out_shape=[jax.ShapeDtypeStruct((groups, S5_BLOCK, S5_BLOCK), BF16),
                   jax.ShapeDtypeStruct((groups, S5_PAIR, S5_BLOCK), BF16),
                   jax.ShapeDtypeStruct((groups, S5_BLOCK, S5_PAIR), BF16),
                   jax.ShapeDtypeStruct((groups, S5_BLOCK, S5_PAIR), BF16),
                   jax.ShapeDtypeStruct((groups // 2, 1, S5_PAIR), F32),
                   jax.ShapeDtypeStruct((groups // 2, 1, S5_PAIR), F32)],
        compiler_params=_params("arbitrary"),
        name="s5_prep",
    )(pairs(lam_re), pairs(lam_im), pairs(jnp.repeat(log_dt, state)),
      bm_re.reshape(groups // 2, S5_PAIR, S5_GROUP), bm_im.reshape(groups // 2, S5_PAIR, S5_GROUP),
      cm_re, cm_im, cpad(cm_re), cpad(cm_im), jnp.tile(d_skip.reshape(groups, 1, S5_GROUP), (1, 1, S5_CHUNK)))


def _s5_core_kernel(ut_ref, t_ref, wb_ref, wcr_ref, wci_ref, dre_ref, dim_ref, o_ref,
                    elr_ref, eli_ref, epr_ref, epi_ref):
    bsz, gb, _, r = ut_ref.shape
    for b in range(bsz):
        for p in range(gb // 2):
            parts = [jnp.dot(wb_ref[2 * p + k], ut_ref[b, 2 * p + k], preferred_element_type=F32) for k in range(2)]
            lanes = slice(p * S5_PAIR, (p + 1) * S5_PAIR)
            elr_ref[b, :, lanes] = jnp.concatenate([parts[0][:S5_STATE], parts[1][:S5_STATE]], axis=0).T
            eli_ref[b, :, lanes] = jnp.concatenate([parts[0][S5_STATE:], parts[1][S5_STATE:]], axis=0).T
    dre = jnp.concatenate([dre_ref[p] for p in range(gb // 2)], axis=1)
    dim = jnp.concatenate([dim_ref[p] for p in range(gb // 2)], axis=1)

    def step(i, carry):
        out = []
        for b in range(bsz):
            er, ei = carry[b]
            epr_ref[b, pl.ds(i, 1), :] = er
            epi_ref[b, pl.ds(i, 1), :] = ei
            lr = elr_ref[b, pl.ds(i, 1), :]
            li = eli_ref[b, pl.ds(i, 1), :]
            out.append((dre * er - dim * ei + lr, dre * ei + dim * er + li))
        return tuple(out)

    zero = jnp.zeros((1, gb * S5_STATE), F32)
    lax.fori_loop(0, r, step, tuple((zero, zero) for _ in range(bsz)), unroll=S5_SCAN_UNROLL)
    for b in range(bsz):
        for g in range(gb):
            lanes = slice((g // 2) * S5_PAIR, (g // 2 + 1) * S5_PAIR)
            y = jnp.dot(t_ref[g], ut_ref[b, g], preferred_element_type=F32)
            y = y + _nt_dot(wcr_ref[g], epr_ref[b, :, lanes].astype(BF16))
            y = y + _nt_dot(wci_ref[g], epi_ref[b, :, lanes].astype(BF16))
            o_ref[b, g] = _gelu(y).astype(BF16)


def _s5_core(ut, tmat, wb, wcr, wci, dre, dim):
    bsz, groups, blk, r = ut.shape
    gb = _tile(groups, 8)
    g3 = lambda n, shape: pl.BlockSpec((n,) + shape, lambda i: (i, 0, 0))
    return pl.pallas_call(
        _s5_core_kernel,
        grid=(groups // gb,),
        in_specs=[pl.BlockSpec((bsz, gb, blk, r), lambda i: (0, i, 0, 0)),
                  g3(gb, (blk, blk)), g3(gb, (S5_PAIR, blk)), g3(gb, (blk, S5_PAIR)), g3(gb, (blk, S5_PAIR)),
                  g3(gb // 2, (1, S5_PAIR)), g3(gb // 2, (1, S5_PAIR))],
        out_specs=pl.BlockSpec((bsz, gb, blk, r), lambda i: (0, i, 0, 0)),
        out_shape=jax.ShapeDtypeStruct(ut.shape, BF16),
        scratch_shapes=[pltpu.VMEM((bsz, r, gb * S5_STATE), F32) for _ in range(4)],
        compiler_params=_params("arbitrary"),
        name="s5_core",
    )(ut, tmat, wb, wcr, wci, dre, dim)


def _s5_out_kernel(x_ref, y_ref, wv_ref, wg_ref, bv_ref, bg_ref, o_ref):
    groups, gsz, r = y_ref.shape
    y = y_ref[...].reshape(groups * gsz, r).astype(F32).T.astype(BF16)
    val = jnp.dot(y, wv_ref[...], preferred_element_type=F32) + bv_ref[...]
    gate = jnp.dot(y, wg_ref[...], preferred_element_type=F32) + bg_ref[...]
    o_ref[...] = x_ref[...] + val * jax.nn.sigmoid(gate)


def _s5_out(xt, yt, w_out, b_out):
    bsz, _, r, d = xt.shape
    width = w_out.shape[0]
    groups = width // S5_GROUP
    b2 = b_out.reshape(1, 2 * d)
    return pl.pallas_call(
        _s5_out_kernel,
        grid=(bsz, S5_CHUNK),
        in_specs=[
            pl.BlockSpec((None, None, r, d), lambda b, t: (b, t, 0, 0)),
            pl.BlockSpec((None, groups, None, S5_GROUP, r), lambda b, t: (b, 0, t, 0, 0)),
            _once((width, d), lambda b, t: (0, 0)),
            _once((width, d), lambda b, t: (0, 1)),
            pl.BlockSpec((1, d), lambda b, t: (0, 0)),
            pl.BlockSpec((1, d), lambda b, t: (0, 1)),
        ],
        out_specs=pl.BlockSpec((None, None, r, d), lambda b, t: (b, t, 0, 0)),
        out_shape=jax.ShapeDtypeStruct(xt.shape, F32),
        compiler_params=_params("arbitrary", "arbitrary"),
        name="s5_out",
    )(xt, yt, w_out, w_out, b2, b2)


def _s5_mixer(x, seq, g, w_in_t, lam_re, lam_im, log_dt, bm_re, bm_im, cm_re, cm_im, d_skip, w_out, b_out):
    n, d = x.shape
    bsz = n // seq
    r = seq // S5_CHUNK
    groups = w_in_t.shape[0] // S5_GROUP
    xt = x.reshape(bsz, S5_CHUNK, r, d)
    ut = _s5_in(xt, g, w_in_t)
    tmat, wb, wcr, wci, dre, dim = _s5_prep(lam_re, lam_im, log_dt, bm_re, bm_im, cm_re, cm_im, d_skip)
    yt = _s5_core(ut.reshape(bsz, groups, S5_BLOCK, r), tmat, wb, wcr, wci, dre, dim)
    out = _s5_out(xt, yt.reshape(bsz, groups, S5_CHUNK, S5_GROUP, r), w_out, b_out)
    return out.reshape(n, d)


def kernel(x, mem, g_mix, g_xattn, g_mem, g_ff, g_final, a_w_in, a_b_in, a_ln_g, a_ln_b, a_w_s, a_b_s, a_w_out, b_w_in, b_lam_re, b_lam_im, b_log_dt, b_bm_re, b_bm_im, b_cm_re, b_cm_im, b_d, b_w_out, b_b_out, x_w_q, x_w_kv, x_w_o, f_w_up, f_w_down):
    bsz, seq, d = x.shape
    depth = g_mix.shape[0]
    assert seq % GMLP_CHUNK == 0 and seq % S5_CHUNK == 0 and d % ATTN_HEADS == 0
    h = x.reshape(bsz * seq, d)
    mem2 = mem.reshape(-1, d)
    tmajor = False
    attn_jobs = lambda i: [(x_w_q, i, False), (x_w_o, i, False), (x_w_kv, i, False)]
    ready = ()
    for i in range(depth):
        j = i // N_MIXERS
        if i % N_MIXERS == 0:
            assert not tmajor
            z, (w_out, w_q, w_o, w_kv) = _gmlp_in(h, g_mix[i], a_w_in, j, a_b_in[j], a_ln_g[j], a_ln_b[j],
                                                  [(a_w_out, j, False)] + attn_jobs(i))
            h = _gmlp_out(h, z, a_w_s[j], a_b_s[j], w_out)
        else:
            assert tmajor
            w_in_t, w_out, w_q, w_o, w_kv = ready
            h = _s5_mixer(h, seq, g_mix[i], w_in_t, b_lam_re[j], b_lam_im[j], b_log_dt[j], b_bm_re[j],
                          b_bm_im[j], b_cm_re[j], b_cm_im[j], b_d[j], w_out, b_b_out[j])
        kv = _mem_kv(mem2, g_mem[i], w_kv)
        h = _attention(h, g_xattn[i], w_q, kv, w_o, seq)
        last = i == depth - 1
        next_is_s5 = (not last) and (i + 1) % N_MIXERS == 1
        jobs = [(b_w_in, (i + 1) // N_MIXERS, True), (b_w_out, (i + 1) // N_MIXERS, False)] + attn_jobs(i + 1) \
            if next_is_s5 else []
        h, ready = _mlp(h, seq, g_ff[i], f_w_up, f_w_down, i, tmajor, next_is_s5,
                        g_final if last else None, jobs)
        tmajor = next_is_s5
    return h.reshape(bsz, seq, d)
```

```python
import functools

import jax
import jax.numpy as jnp
from jax import lax
from jax.experimental import pallas as pl
from jax.experimental.pallas import tpu as pltpu

F32 = jnp.float32
BF16 = jnp.bfloat16

EPS = 1e-6
LAM_RE_MAX = -1e-4
N_MIXERS = 2

LANES = 128
SUBLANES = 8
GMLP_CHUNK = 128
GMLP_HEAD_DIM = 128
ATTN_HEADS = 4
S5_GROUP = 16
S5_STATE = 64
S5_CHUNK = 16
S5_BLOCK = S5_CHUNK * S5_GROUP
S5_PAIR = 2 * S5_STATE
S5_POW_ROWS = 24
S5_STEPS = 2

STAGE_ROWS = 256
V7X_VMEM_LIMIT_BYTES = 60 * 1024 * 1024


def _params(*semantics):
    return pltpu.CompilerParams(dimension_semantics=semantics, vmem_limit_bytes=V7X_VMEM_LIMIT_BYTES)


def _once(block_shape, index_map):
    return pl.BlockSpec(block_shape, index_map, pipeline_mode=pl.Buffered(1))


def _tile(n, want):
    t = min(n, want)
    assert n % t == 0, (n, want)
    return t


def _rms(x, g):
    ms = jnp.mean(x * x, axis=-1, keepdims=True)
    return x * lax.rsqrt(ms + EPS) * g


def _gelu(x):
    return 0.5 * x * (1.0 + lax.erf(x * (2.0 ** -0.5)))


def _nt_dot(a, b):
    return lax.dot_general(a, b, (((1,), (1,)), ((), ())), preferred_element_type=F32)


def _row(a):
    return a.reshape(1, -1)


def _work_index(step, n_stage):
    return jnp.maximum(step - n_stage, 0)


def _stage_rows(step, src_ref, dst_ref):
    rows = pl.ds(pl.multiple_of(step * STAGE_ROWS, STAGE_ROWS), STAGE_ROWS)
    dst_ref[rows, :] = src_ref[...].astype(BF16)


def _cast_plan(jobs, n_steps, step_of):
    arrays, in_specs, out_specs, out_shapes = [], [], [], []
    for w_all, layer, transpose in jobs:
        _, k, n = w_all.shape
        n_blocks = min(n_steps, k // LANES) if transpose else n_steps
        assert n_steps % n_blocks == 0 and k % n_blocks == 0, (k, n_steps)
        rows, per = k // n_blocks, n_steps // n_blocks
        blk = lambda *ids, per=per: step_of(*ids) // per
        arrays.append(w_all)
        in_specs.append(pl.BlockSpec((None, rows, n), lambda *ids, layer=layer, blk=blk: (layer, blk(*ids), 0)))
        if transpose:
            out_specs.append(pl.BlockSpec((n, rows), lambda *ids, blk=blk: (0, blk(*ids))))
            out_shapes.append(jax.ShapeDtypeStruct((n, k), BF16))
        else:
            out_specs.append(pl.BlockSpec((rows, n), lambda *ids, blk=blk: (blk(*ids), 0)))
            out_shapes.append(jax.ShapeDtypeStruct((k, n), BF16))
    return arrays, in_specs, out_specs, out_shapes


def _cast_blocks(src_refs, dst_refs, jobs):
    for src, dst, (_, _, transpose) in zip(src_refs, dst_refs, jobs):
        v = src[...]
        dst[...] = (v.T if transpose else v).astype(BF16)


def _gmlp_in_kernel(*refs, n_stage, jobs):
    nj = len(jobs)
    x_ref, g_ref, w_ref, b_ref, lng_ref, lnb_ref = refs[:6]
    cast_src = refs[6:6 + nj]
    o_ref = refs[6 + nj]
    cast_dst = refs[7 + nj:7 + 2 * nj]
    wb_ref = refs[7 + 2 * nj]
    s = pl.program_id(1)

    @pl.when(s < n_stage)
    def _():
        _stage_rows(s, w_ref, wb_ref)

    @pl.when(s >= n_stage)
    def _():
        _cast_blocks(cast_src, cast_dst, jobs)
        h = _rms(x_ref[...], g_ref[...]).astype(BF16)
        z = _gelu(jnp.dot(h, wb_ref[...], preferred_element_type=F32) + b_ref[...])

        @pl.when(pl.program_id(0) == 0)
        def _():
            o_ref[...] = z.astype(o_ref.dtype)

        @pl.when(pl.program_id(0) == 1)
        def _():
            mu = jnp.mean(z, axis=-1, keepdims=True)
            zc = z - mu
            var = jnp.mean(zc * zc, axis=-1, keepdims=True)
            o_ref[...] = (zc * lax.rsqrt(var + EPS) * lng_ref[...] + lnb_ref[...]).astype(o_ref.dtype)


def _gmlp_in(x, g, w_all, layer, b, ln_g, ln_b, jobs):
    n, d = x.shape
    width = w_all.shape[2] // 2
    tm = _tile(n, 512)
    n_tiles = n // tm
    n_stage = d // STAGE_ROWS
    work = lambda s: _work_index(s, n_stage)
    cast_arrays, cast_in, cast_out, cast_shapes = _cast_plan(jobs, 2 * n_tiles, lambda j, s: j * n_tiles + work(s))
    outs = pl.pallas_call(
        functools.partial(_gmlp_in_kernel, n_stage=n_stage, jobs=jobs),
        grid=(2, n_stage + n_tiles),
        in_specs=[
            pl.BlockSpec((tm, d), lambda j, s: (work(s), 0)),
            pl.BlockSpec((1, d), lambda j, s: (0, 0)),
            pl.BlockSpec((None, STAGE_ROWS, width), lambda j, s: (layer, jnp.minimum(s, n_stage - 1), j)),
            pl.BlockSpec((1, width), lambda j, s: (0, j)),
            pl.BlockSpec((1, width), lambda j, s: (0, 0)),
            pl.BlockSpec((1, width), lambda j, s: (0, 0)),
        ] + cast_in,
        out_specs=[pl.BlockSpec((tm, width), lambda j, s: (work(s), j))] + cast_out,
        out_shape=[jax.ShapeDtypeStruct((n, 2 * width), BF16)] + cast_shapes,
        scratch_shapes=[pltpu.VMEM((d, width), BF16)],
        compiler_params=_params("arbitrary", "arbitrary"),
        name="gmlp_in",
    )(x, _row(g), w_all, _row(b), _row(ln_g), _row(ln_b), *cast_arrays)
    return outs[0], outs[1:]


def _gmlp_out_kernel(x_ref, u_ref, v_ref, ws_ref, bs_ref, w_ref, o_ref, gated_ref):
    tm, width = u_ref.shape
    heads = ws_ref.shape[0]
    t_idx = lax.broadcasted_iota(jnp.int32, (GMLP_CHUNK, GMLP_CHUNK), 0)
    s_idx = lax.broadcasted_iota(jnp.int32, (GMLP_CHUNK, GMLP_CHUNK), 1)
    causal = t_idx >= s_idx
    for h in range(heads):
        w_h = jnp.where(causal, ws_ref[h], 0.0).astype(BF16)
        b_h = bs_ref[h]
        cols = slice(h * GMLP_HEAD_DIM, (h + 1) * GMLP_HEAD_DIM)
        for c in range(tm // GMLP_CHUNK):
            rows = slice(c * GMLP_CHUNK, (c + 1) * GMLP_CHUNK)
            sg = jnp.dot(w_h, v_ref[rows, cols], preferred_element_type=F32) + b_h
            gated_ref[rows, cols] = (u_ref[rows, cols].astype(F32) * sg).astype(BF16)
    o_ref[...] = x_ref[...] + jnp.dot(gated_ref[...], w_ref[...], preferred_element_type=F32)


def _gmlp_out(x, z, w_s, b_s, w_out):
    n, d = x.shape
    width = z.shape[1] // 2
    heads = w_s.shape[0]
    tm = _tile(n, 512)
    return pl.pallas_call(
        _gmlp_out_kernel,
        grid=(n // tm,),
        in_specs=[
            pl.BlockSpec((tm, d), lambda i: (i, 0)),
            pl.BlockSpec((tm, width), lambda i: (i, 0)),
            pl.BlockSpec((tm, width), lambda i: (i, 1)),
            _once((heads, GMLP_CHUNK, GMLP_CHUNK), lambda i: (0, 0, 0)),
            _once((heads, GMLP_CHUNK, 1), lambda i: (0, 0, 0)),
            _once((width, d), lambda i: (0, 0)),
        ],
        out_specs=pl.BlockSpec((tm, d), lambda i: (i, 0)),
        out_shape=jax.ShapeDtypeStruct((n, d), F32),
        scratch_shapes=[pltpu.VMEM((tm, width), BF16)],
        compiler_params=_params("arbitrary"),
        name="gmlp_out",
    )(x, z, z, w_s, b_s.reshape(heads, GMLP_CHUNK, 1), w_out)


def _mem_kv_kernel(x_ref, g_ref, w_ref, o_ref, h_ref):
    @pl.when(pl.program_id(0) == 0)
    def _():
        h_ref[...] = _rms(x_ref[...], g_ref[...]).astype(BF16)

    o_ref[...] = jnp.dot(h_ref[...], w_ref[...], preferred_element_type=F32).astype(o_ref.dtype)


def _mem_kv(mem, g, w_kv):
    n, d = mem.shape
    m = w_kv.shape[1]
    tn = _tile(m, 2048)
    return pl.pallas_call(
        _mem_kv_kernel,
        grid=(m // tn,),
        in_specs=[
            _once((n, d), lambda j: (0, 0)),
            _once((1, d), lambda j: (0, 0)),
            pl.BlockSpec((d, tn), lambda j: (0, j)),
        ],
        out_specs=pl.BlockSpec((n, tn), lambda j: (0, j)),
        out_shape=jax.ShapeDtypeStruct((n, m), BF16),
        scratch_shapes=[pltpu.VMEM((n, d), BF16)],
        compiler_params=_params("arbitrary"),
        name="mem_kv",
    )(mem, _row(g), w_kv)


def _attn_kernel(x_ref, g_ref, wq_ref, kv_ref, wo_ref, o_ref, heads_ref):
    x = x_ref[...]
    d = x.shape[1]
    hd = d // ATTN_HEADS
    q = jnp.dot(_rms(x, g_ref[...]).astype(BF16), wq_ref[...], preferred_element_type=F32).astype(BF16)
    scale = hd ** -0.5
    for h in range(ATTN_HEADS):
        cols = slice(h * hd, (h + 1) * hd)
        k_h = kv_ref[:, h * hd:(h + 1) * hd]
        v_h = kv_ref[:, d + h * hd:d + (h + 1) * hd]
        sc = _nt_dot(q[:, cols], k_h) * scale
        p = jnp.exp(sc - jnp.max(sc, axis=-1, keepdims=True))
        p = p / jnp.sum(p, axis=-1, keepdims=True)
        heads_ref[:, cols] = jnp.dot(p.astype(BF16), v_h, preferred_element_type=F32).astype(BF16)
    o_ref[...] = x + jnp.dot(heads_ref[...], wo_ref[...], preferred_element_type=F32)


def _attention(x, g, w_q, kv, w_o, seq):
    n, d = x.shape
    mem_len = kv.shape[0] // (n // seq)
    tm = _tile(seq, 512)
    per_batch = seq // tm
    return pl.pallas_call(
        _attn_kernel,
        grid=(n // tm,),
        in_specs=[
            pl.BlockSpec((tm, d), lambda i: (i, 0)),
            pl.BlockSpec((1, d), lambda i: (0, 0)),
            _once((d, d), lambda i: (0, 0)),
            pl.BlockSpec((mem_len, 2 * d), lambda i: (i // per_batch, 0)),
            _once((d, d), lambda i: (0, 0)),
        ],
        out_specs=pl.BlockSpec((tm, d), lambda i: (i, 0)),
        out_shape=jax.ShapeDtypeStruct((n, d), F32),
        scratch_shapes=[pltpu.VMEM((tm, d), BF16)],
        compiler_params=_params("arbitrary"),
        name="attention",
    )(x, _row(g), w_q, kv, w_o)


def _mlp_kernel(*refs, reorder, final_norm, per_batch, n_tiles, jobs):
    nj = len(jobs)
    x_hbm, g_ref, gf_ref, wup_ref, wdn_ref = refs[:5]
    cast_src = refs[5:5 + nj]
    o_ref = refs[5 + nj]
    cast_dst = refs[6 + nj:6 + 2 * nj]
    xbuf, h_ref, sem = refs[6 + 2 * nj:]
    i, j = pl.program_id(0), pl.program_id(1)
    tm, d = h_ref.shape
    rb = tm // S5_CHUNK

    def x_tile(tile):
        if reorder == "none":
            return [pltpu.make_async_copy(x_hbm.at[pl.ds(tile * tm, tm)], xbuf, sem)]
        if reorder == "to_tmajor":
            return [pltpu.make_async_copy(x_hbm.at[pl.ds(tile * rb, rb), t, :], xbuf.at[t], sem)
                    for t in range(S5_CHUNK)]
        row0 = (tile // per_batch) * S5_CHUNK
        return [pltpu.make_async_copy(x_hbm.at[row0 + t, pl.ds((tile % per_batch) * rb, rb), :], xbuf.at[:, t, :], sem)
                for t in range(S5_CHUNK)]

    @pl.when(j == 0)
    def _():
        @pl.when(i == 0)
        def _():
            for cp in x_tile(0):
                cp.start()

        for cp in x_tile(i):
            cp.wait()
        o_ref[...] = xbuf[...]
        h_ref[...] = _rms(xbuf[...].reshape(tm, d), g_ref[...]).astype(BF16)

        @pl.when(i + 1 < n_tiles)
        def _():
            for cp in x_tile(i + 1):
                cp.start()

    _cast_blocks(cast_src, cast_dst, jobs)
    a = jnp.dot(h_ref[...], wup_ref[...].astype(BF16), preferred_element_type=F32)
    a = jnp.square(jnp.maximum(a, 0.0)).astype(BF16)
    o_ref[...] += jnp.dot(a, wdn_ref[...].astype(BF16), preferred_element_type=F32).reshape(o_ref.shape)

    if final_norm:
        @pl.when(j == pl.num_programs(1) - 1)
        def _():
            o_ref[...] = _rms(o_ref[...], gf_ref[...])


def _mlp(x, seq, g, wup_all, wdn_all, layer, tmajor_in, tmajor_out, g_final=None, jobs=()):
    n, d = x.shape
    ff = wup_all.shape[2]
    tm = _tile(seq, 1024)
    tf = _tile(ff, 512)
    per_batch = seq // tm
    n_tiles, n_ff = n // tm, ff // tf
    rb = tm // S5_CHUNK
    r = seq // S5_CHUNK
    nat_out = ((n // S5_CHUNK, S5_CHUNK, d), (rb, S5_CHUNK, d), lambda i, j: (i, 0, 0))
    tmj_out = ((n // seq, S5_CHUNK, r, d), (None, S5_CHUNK, rb, d), lambda i, j: (i // per_batch, 0, i % per_batch, 0))
    flat_out = ((n, d), (tm, d), lambda i, j: (i, 0))
    if tmajor_in == tmajor_out:
        reorder, src_view, dst = "none", (n, d), flat_out
    elif tmajor_out:
        reorder, src_view, dst = "to_tmajor", (n // S5_CHUNK, S5_CHUNK, d), tmj_out
    else:
        reorder, src_view, dst = "from_tmajor", (n // seq * S5_CHUNK, r, d), nat_out
    tile_shape = tuple(s for s in dst[1] if s is not None)
    gf = g if g_final is None else g_final
    cast_arrays, cast_in, cast_out, cast_shapes = _cast_plan(jobs, n_tiles * n_ff, lambda i, j: i * n_ff + j)
    outs = pl.pallas_call(
        functools.partial(_mlp_kernel, reorder=reorder, final_norm=g_final is not None,
                          per_batch=per_batch, n_tiles=n_tiles, jobs=jobs),
        grid=(n_tiles, n_ff),
        in_specs=[
            pl.BlockSpec(memory_space=pl.ANY),
            pl.BlockSpec((1, d), lambda i, j: (0, 0)),
            pl.BlockSpec((1, d), lambda i, j: (0, 0)),
            pl.BlockSpec((None, d, tf), lambda i, j: (layer, 0, j)),
            pl.BlockSpec((None, tf, d), lambda i, j: (layer, j, 0)),
        ] + cast_in,
        out_specs=[pl.BlockSpec(dst[1], dst[2])] + cast_out,
        out_shape=[jax.ShapeDtypeStruct(dst[0], F32)] + cast_shapes,
        scratch_shapes=[pltpu.VMEM(tile_shape, F32), pltpu.VMEM((tm, d), BF16), pltpu.SemaphoreType.DMA(())],
        compiler_params=_params("arbitrary", "arbitrary"),
        name="mlp",
    )(x.reshape(src_view), _row(g), _row(gf), wup_all, wdn_all, *cast_arrays)
    return outs[0].reshape(n, d), outs[1:]


def _s5_in_kernel(x_ref, g_ref, wt_ref, o_ref):
    nt, r, d = x_ref.shape
    groups = o_ref.shape[0]
    h = _rms(x_ref[...].reshape(nt * r, d), g_ref[...]).astype(BF16)
    ut = _nt_dot(wt_ref[...], h).astype(BF16)
    for t in range(nt):
        o_ref[:, t] = ut[:, t * r:(t + 1) * r].reshape(groups, S5_GROUP, r)


def _s5_in(xt, g, w_in_t):
    bsz, _, r, d = xt.shape
    width = w_in_t.shape[0]
    groups = width // S5_GROUP
    return pl.pallas_call(
        _s5_in_kernel,
        grid=(bsz, S5_CHUNK // S5_STEPS),
        in_specs=[
            pl.BlockSpec((None, S5_STEPS, r, d), lambda b, t: (b, t, 0, 0)),
            pl.BlockSpec((1, d), lambda b, t: (0, 0)),
            _once((width, d), lambda b, t: (0, 0)),
        ],
        out_specs=pl.BlockSpec((None, groups, S5_STEPS, S5_GROUP, r), lambda b, t: (b, 0, t, 0, 0)),
        out_shape=jax.ShapeDtypeStruct((bsz, groups, S5_CHUNK, S5_GROUP, r), BF16),
        compiler_params=_params("arbitrary", "arbitrary"),
        name="s5_in",
    )(xt, _row(g), w_in_t)


def _cmul(ar, ai, br, bi):
    return ar * br - ai * bi, ar * bi + ai * br


def _s5_prep_kernel(lr_ref, li_ref, ldt_ref, bre_ref, bim_ref, cre_ref, cim_ref, crp_ref, cip_ref, dsk_ref,
                    t_ref, wb_ref, wcr_ref, wci_ref, dre_ref, dim_ref):
    npair = lr_ref.shape[0]
    idx = lambda shape, dim: lax.broadcasted_iota(jnp.int32, shape, dim)
    rev_t = (idx((S5_PAIR, S5_BLOCK), 0) == S5_CHUNK - 1 - idx((S5_PAIR, S5_BLOCK), 1) // S5_GROUP).astype(BF16)
    tile_c = (idx((S5_GROUP, S5_BLOCK), 0) == idx((S5_GROUP, S5_BLOCK), 1) % S5_GROUP).astype(BF16)
    skip_mask = ((idx((S5_GROUP, S5_BLOCK), 1) // S5_GROUP == S5_CHUNK - 1)
                 & (idx((S5_GROUP, S5_BLOCK), 1) % S5_GROUP == idx((S5_GROUP, S5_BLOCK), 0)))
    k_col = idx((S5_POW_ROWS, 1), 0).astype(F32)
    pad_rows = jnp.zeros((S5_PAIR - S5_POW_ROWS, S5_PAIR), F32)

    def split_dot(x, m):
        hi = x.astype(BF16)
        lo = (x - hi.astype(F32)).astype(BF16)
        return jnp.dot(hi, m, preferred_element_type=F32) + jnp.dot(lo, m, preferred_element_type=F32)

    def rows_x16(tab):
        return jnp.concatenate([jnp.broadcast_to(tab[t + 1:t + 2], (S5_GROUP, S5_PAIR)) for t in range(S5_CHUNK)],
                               axis=0)

    for p in range(npair):
        lr = jnp.minimum(lr_ref[p], LAM_RE_MAX)
        li = li_ref[p]
        dt = jnp.exp(ldt_ref[p])
        zr, zi = lr * dt, li * dt
        mag = jnp.exp(k_col * zr)
        pw_r, pw_i = mag * jnp.cos(k_col * zi), mag * jnp.sin(k_col * zi)
        ar, ai = pw_r[1:2], pw_i[1:2]
        den = lr * lr + li * li
        qr = ((ar - 1.0) * lr + ai * li) / den
        qi = (ai * lr - (ar - 1.0) * li) / den
        dre_ref[p] = pw_r[S5_CHUNK:S5_CHUNK + 1]
        dim_ref[p] = pw_i[S5_CHUNK:S5_CHUNK + 1]
        nx_r, nx_i = rows_x16(pw_r), rows_x16(pw_i)
        sc_r, sc_i = _cmul(pw_r, pw_i, qr, qi)
        sp_r = split_dot(jnp.concatenate([sc_r, pad_rows], axis=0).T, rev_t)
        sp_i = split_dot(jnp.concatenate([sc_i, pad_rows], axis=0).T, rev_t)
        wb_r, wb_i = _cmul(sp_r, sp_i, split_dot(bre_ref[p], tile_c), split_dot(bim_ref[p], tile_c))
        for k in range(2):
            g = 2 * p + k
            rows = slice(k * S5_STATE, (k + 1) * S5_STATE)
            wb_g = jnp.concatenate([wb_r[rows], wb_i[rows]], axis=0).astype(BF16)
            wb_ref[g] = wb_g
            krev = split_dot(cre_ref[g], wb_g[:S5_STATE]) - split_dot(cim_ref[g], wb_g[S5_STATE:])
            krev = krev + jnp.where(skip_mask, dsk_ref[g], 0.0)
            kext = jnp.concatenate([krev, jnp.zeros_like(krev)], axis=1)
            for tp in range(S5_CHUNK):
                off = (S5_CHUNK - 1 - tp) * S5_GROUP
                win = kext if off == 0 else pltpu.roll(kext, 2 * S5_BLOCK - off, axis=1)
                t_ref[g, tp * S5_GROUP:(tp + 1) * S5_GROUP, :] = win[:, :S5_BLOCK].astype(BF16)
            c_r = jnp.concatenate([crp_ref[g]] * S5_CHUNK, axis=0)
            c_i = jnp.concatenate([cip_ref[g]] * S5_CHUNK, axis=0)
            wcr_ref[g] = (c_r * nx_r - c_i * nx_i).astype(BF16)
            wci_ref[g] = (-(c_r * nx_i + c_i * nx_r)).astype(BF16)


def _s5_prep(lam_re, lam_im, log_dt, bm_re, bm_im, cm_re, cm_im, d_skip):
    groups, state = lam_re.shape
    assert state == S5_STATE and groups % 2 == 0
    gb = _tile(groups, 8)
    pairs = lambda a: a.reshape(groups // 2, 1, S5_PAIR)
    half = jnp.eye(2, dtype=F32)[jnp.arange(groups) % 2]
    cpad = lambda a: (a[:, :, None, :] * half[:, None, :, None]).reshape(groups, S5_GROUP, S5_PAIR)
    g3 = lambda n, shape: pl.BlockSpec((n,) + shape, lambda i: (i, 0, 0))
    return pl.pallas_call(
        _s5_prep_kernel,
        grid=(groups // gb,),
        in_specs=[g3(gb // 2, (1, S5_PAIR)), g3(gb // 2, (1, S5_PAIR)), g3(gb // 2, (1, S5_PAIR)),
                  g3(gb // 2, (S5_PAIR, S5_GROUP)), g3(gb // 2, (S5_PAIR, S5_GROUP)),
                  g3(gb, (S5_GROUP, state)), g3(gb, (S5_GROUP, state)),
                  g3(gb, (S5_GROUP, S5_PAIR)), g3(gb, (S5_GROUP, S5_PAIR)), g3(gb, (1, S5_BLOCK))],
        out_specs=[g3(gb, (S5_BLOCK, S5_BLOCK)), g3(gb, (S5_PAIR, S5_BLOCK)), g3(gb, (S5_BLOCK, S5_PAIR)),
                   g3(gb, (S5_BLOCK, S5_PAIR)), g3(gb // 2, (1, S5_PAIR)), g3(gb // 2, (1, S5_PAIR))],
        out_shape=[jax.ShapeDtypeStruct((groups, S5_BLOCK, S5_BLOCK), BF16),
                   jax.ShapeDtypeStruct((groups, S5_PAIR, S5_BLOCK), BF16),
                   jax.ShapeDtypeStruct((groups, S5_BLOCK, S5_PAIR), BF16),
                   jax.ShapeDtypeStruct((groups, S5_BLOCK, S5_PAIR), BF16),
                   jax.ShapeDtypeStruct((groups // 2, 1, S5_PAIR), F32),
                   jax.ShapeDtypeStruct((groups // 2, 1, S5_PAIR), F32)],
        compiler_params=_params("arbitrary"),
        name="s5_prep",
    )(pairs(lam_re), pairs(lam_im), pairs(jnp.repeat(log_dt, state)),
      bm_re.reshape(groups // 2, S5_PAIR, S5_GROUP), bm_im.reshape(groups // 2, S5_PAIR, S5_GROUP),
      cm_re, cm_im, cpad(cm_re), cpad(cm_im), jnp.tile(d_skip.reshape(groups, 1, S5_GROUP), (1, 1, S5_CHUNK)))


def _s5_core_kernel(ut_ref, t_ref, wb_ref, wcr_ref, wci_ref, dre_ref, dim_ref, o_ref,
                    elr_ref, eli_ref, epr_ref, epi_ref):
    bsz, gb, _, r = ut_ref.shape
    for b in range(bsz):
        for p in range(gb // 2):
            parts = [jnp.dot(wb_ref[2 * p + k], ut_ref[b, 2 * p + k], preferred_element_type=F32) for k in range(2)]
            lanes = slice(p * S5_PAIR, (p + 1) * S5_PAIR)
            elr_ref[b, :, lanes] = jnp.concatenate([parts[0][:S5_STATE], parts[1][:S5_STATE]], axis=0).T
            eli_ref[b, :, lanes] = jnp.concatenate([parts[0][S5_STATE:], parts[1][S5_STATE:]], axis=0).T
    lanes_all = gb * S5_STATE
    d1 = (jnp.concatenate([dre_ref[p] for p in range(gb // 2)], axis=1),
          jnp.concatenate([dim_ref[p] for p in range(gb // 2)], axis=1))
    d2 = _cmul(*d1, *d1)
    d4 = _cmul(*d2, *d2)
    d6 = _cmul(*d4, *d2)
    pows = [d1, d2, _cmul(*d2, *d1), d4, _cmul(*d4, *d1), d6, _cmul(*d6, *d1), _cmul(*d4, *d4)]
    dp_r = jnp.concatenate([p[0] for p in pows], axis=0)
    dp_i = jnp.concatenate([p[1] for p in pows], axis=0)
    sub = lax.broadcasted_iota(jnp.int32, (SUBLANES, lanes_all), 0)

    def shift_rows(x, n, fill):
        return jnp.where(sub >= n, pltpu.roll(x, n, axis=0), fill)

    def tile_step(k, carry):
        rows = pl.ds(pl.multiple_of(k * SUBLANES, SUBLANES), SUBLANES)
        out = []
        for b in range(bsz):
            cr, ci = carry[b]
            lr, li = elr_ref[b, rows, :], eli_ref[b, rows, :]
            for n, (pr, pi) in ((1, d1), (2, d2), (4, d4)):
                sr, si = shift_rows(lr, n, 0.0), shift_rows(li, n, 0.0)
                lr, li = lr + pr * sr - pi * si, li + pr * si + pi * sr
            er = lr + dp_r * cr - dp_i * ci
            ei = li + dp_r * ci + dp_i * cr
            epr_ref[b, rows, :] = shift_rows(er, 1, cr)
            epi_ref[b, rows, :] = shift_rows(ei, 1, ci)
            out.append((jnp.broadcast_to(er[SUBLANES - 1:], er.shape), jnp.broadcast_to(ei[SUBLANES - 1:], ei.shape)))
        return tuple(out)

    zero = jnp.zeros((SUBLANES, lanes_all), F32)
    lax.fori_loop(0, r // SUBLANES, tile_step, tuple((zero, zero) for _ in range(bsz)))
    for b in range(bsz):
        for g in range(gb):
            lanes = slice((g // 2) * S5_PAIR, (g // 2 + 1) * S5_PAIR)
            y = jnp.dot(t_ref[g], ut_ref[b, g], preferred_element_type=F32)
            y = y + _nt_dot(wcr_ref[g], epr_ref[b, :, lanes].astype(BF16))
            y = y + _nt_dot(wci_ref[g], epi_ref[b, :, lanes].astype(BF16))
            o_ref[b, g] = _gelu(y).astype(BF16)


def _s5_core(ut, tmat, wb, wcr, wci, dre, dim):
    bsz, groups, blk, r = ut.shape
    gb = _tile(groups, 8)
    g3 = lambda n, shape: pl.BlockSpec((n,) + shape, lambda i: (i, 0, 0))
    return pl.pallas_call(
        _s5_core_kernel,
        grid=(groups // gb,),
        in_specs=[pl.BlockSpec((bsz, gb, blk, r), lambda i: (0, i, 0, 0)),
                  g3(gb, (blk, blk)), g3(gb, (S5_PAIR, blk)), g3(gb, (blk, S5_PAIR)), g3(gb, (blk, S5_PAIR)),
                  g3(gb // 2, (1, S5_PAIR)), g3(gb // 2, (1, S5_PAIR))],
        out_specs=pl.BlockSpec((bsz, gb, blk, r), lambda i: (0, i, 0, 0)),
        out_shape=jax.ShapeDtypeStruct(ut.shape, BF16),
        scratch_shapes=[pltpu.VMEM((bsz, r, gb * S5_STATE), F32) for _ in range(4)],
        compiler_params=_params("arbitrary"),
        name="s5_core",
    )(ut, tmat, wb, wcr, wci, dre, dim)


def _s5_out_kernel(x_ref, y_ref, wv_ref, wg_ref, bv_ref, bg_ref, o_ref):
    groups, nt, gsz, r = y_ref.shape
    y = jnp.concatenate([y_ref[:, t].reshape(groups * gsz, r).astype(F32).T.astype(BF16) for t in range(nt)],
                        axis=0)
    val = jnp.dot(y, wv_ref[...], preferred_element_type=F32) + bv_ref[...]
    gate = jnp.dot(y, wg_ref[...], preferred_element_type=F32) + bg_ref[...]
    o_ref[...] = x_ref[...] + (val * jax.nn.sigmoid(gate)).reshape(o_ref.shape)


def _s5_out(xt, yt, w_out, b_out):
    bsz, _, r, d = xt.shape
    width = w_out.shape[0]
    groups = width // S5_GROUP
    b2 = b_out.reshape(1, 2 * d)
    return pl.pallas_call(
        _s5_out_kernel,
        grid=(bsz, S5_CHUNK // S5_STEPS),
        in_specs=[
            pl.BlockSpec((None, S5_STEPS, r, d), lambda b, t: (b, t, 0, 0)),
            pl.BlockSpec((None, groups, S5_STEPS, S5_GROUP, r), lambda b, t: (b, 0, t, 0, 0)),
            _once((width, d), lambda b, t: (0, 0)),
            _once((width, d), lambda b, t: (0, 1)),
            pl.BlockSpec((1, d), lambda b, t: (0, 0)),
            pl.BlockSpec((1, d), lambda b, t: (0, 1)),
        ],
        out_specs=pl.BlockSpec((None, S5_STEPS, r, d), lambda b, t: (b, t, 0, 0)),
        out_shape=jax.ShapeDtypeStruct(xt.shape, F32),
        compiler_params=_params("arbitrary", "arbitrary"),
        name="s5_out",
    )(xt, yt, w_out, w_out, b2, b2)


def _s5_mixer(x, seq, g, w_in_t, lam_re, lam_im, log_dt, bm_re, bm_im, cm_re, cm_im, d_skip, w_out, b_out):
    n, d = x.shape
    bsz = n // seq
    r = seq // S5_CHUNK
    groups = w_in_t.shape[0] // S5_GROUP
    xt = x.reshape(bsz, S5_CHUNK, r, d)
    ut = _s5_in(xt, g, w_in_t)
    tmat, wb, wcr, wci, dre, dim = _s5_prep(lam_re, lam_im, log_dt, bm_re, bm_im, cm_re, cm_im, d_skip)
    yt = _s5_core(ut.reshape(bsz, groups, S5_BLOCK, r), tmat, wb, wcr, wci, dre, dim)
    out = _s5_out(xt, yt.reshape(bsz, groups, S5_CHUNK, S5_GROUP, r), w_out, b_out)
    return out.reshape(n, d)


def kernel(x, mem, g_mix, g_xattn, g_mem, g_ff, g_final, a_w_in, a_b_in, a_ln_g, a_ln_b, a_w_s, a_b_s, a_w_out, b_w_in, b_lam_re, b_lam_im, b_log_dt, b_bm_re, b_bm_im, b_cm_re, b_cm_im, b_d, b_w_out, b_b_out, x_w_q, x_w_kv, x_w_o, f_w_up, f_w_down):
    bsz, seq, d = x.shape
    depth = g_mix.shape[0]
    assert seq % GMLP_CHUNK == 0 and seq % S5_CHUNK == 0 and d % ATTN_HEADS == 0
    h = x.reshape(bsz * seq, d)
    mem2 = mem.reshape(-1, d)
    tmajor = False
    attn_jobs = lambda i: [(x_w_q, i, False), (x_w_o, i, False), (x_w_kv, i, False)]
    ready = ()
    for i in range(depth):
        j = i // N_MIXERS
        if i % N_MIXERS == 0:
            assert not tmajor
            z, (w_out, w_q, w_o, w_kv) = _gmlp_in(h, g_mix[i], a_w_in, j, a_b_in[j], a_ln_g[j], a_ln_b[j],
                                                  [(a_w_out, j, False)] + attn_jobs(i))
            h = _gmlp_out(h, z, a_w_s[j], a_b_s[j], w_out)
        else:
            assert tmajor
            w_in_t, w_out, w_q, w_o, w_kv = ready
            h = _s5_mixer(h, seq, g_mix[i], w_in_t, b_lam_re[j], b_lam_im[j], b_log_dt[j], b_bm_re[j],
                          b_bm_im[j], b_cm_re[j], b_cm_im[j], b_d[j], w_out, b_b_out[j])
        kv = _mem_kv(mem2, g_mem[i], w_kv)
        h = _attention(h, g_xattn[i], w_q, kv, w_o, seq)
        last = i == depth - 1
        next_is_s5 = (not last) and (i + 1) % N_MIXERS == 1
        jobs = [(b_w_in, (i + 1) // N_MIXERS, True), (b_w_out, (i + 1) // N_MIXERS, False)] + attn_jobs(i + 1) \
            if next_is_s5 else []
        h, ready = _mlp(h, seq, g_ff[i], f_w_up, f_w_down, i, tmajor, next_is_s5,
                        g_final if last else None, jobs)
        tmajor = next_is_s5
    return h.reshape(bsz, seq, d)
```

```python
import functools

import jax
import jax.numpy as jnp
from jax import lax
from jax.experimental import pallas as pl
from jax.experimental.pallas import tpu as pltpu

F32 = jnp.float32
BF16 = jnp.bfloat16

EPS = 1e-6
LAM_RE_MAX = -1e-4
N_MIXERS = 2

LANES = 128
SUBLANES = 8
GMLP_CHUNK = 128
GMLP_HEAD_DIM = 128
ATTN_HEADS = 4
S5_GROUP = 16
S5_STATE = 64
S5_CHUNK = 16
S5_BLOCK = S5_CHUNK * S5_GROUP
S5_PAIR = 2 * S5_STATE
S5_POW_ROWS = 24
S5_STEPS = 2

STAGE_ROWS = 256
V7X_VMEM_LIMIT_BYTES = 60 * 1024 * 1024


def _params(*semantics):
    return pltpu.CompilerParams(dimension_semantics=semantics, vmem_limit_bytes=V7X_VMEM_LIMIT_BYTES)


def _once(block_shape, index_map):
    return pl.BlockSpec(block_shape, index_map, pipeline_mode=pl.Buffered(1))


def _tile(n, want):
    t = min(n, want)
    assert n % t == 0, (n, want)
    return t


def _rms(x, g):
    ms = jnp.mean(x * x, axis=-1, keepdims=True)
    return x * lax.rsqrt(ms + EPS) * g


def _gelu(x):
    return 0.5 * x * (1.0 + lax.erf(x * (2.0 ** -0.5)))


def _nt_dot(a, b):
    return lax.dot_general(a, b, (((1,), (1,)), ((), ())), preferred_element_type=F32)


def _row(a):
    return a.reshape(1, -1)


def _work_index(step, n_stage):
    return jnp.maximum(step - n_stage, 0)


def _stage_rows(step, src_ref, dst_ref):
    rows = pl.ds(pl.multiple_of(step * STAGE_ROWS, STAGE_ROWS), STAGE_ROWS)
    dst_ref[rows, :] = src_ref[...].astype(BF16)


def _cast_plan(jobs, n_steps, step_of):
    arrays, in_specs, out_specs, out_shapes = [], [], [], []
    for w_all, layer, transpose in jobs:
        _, k, n = w_all.shape
        n_blocks = min(n_steps, k // LANES) if transpose else n_steps
        assert n_steps % n_blocks == 0 and k % n_blocks == 0, (k, n_steps)
        rows, per = k // n_blocks, n_steps // n_blocks
        blk = lambda *ids, per=per: step_of(*ids) // per
        arrays.append(w_all)
        in_specs.append(pl.BlockSpec((None, rows, n), lambda *ids, layer=layer, blk=blk: (layer, blk(*ids), 0)))
        if transpose:
            out_specs.append(pl.BlockSpec((n, rows), lambda *ids, blk=blk: (0, blk(*ids))))
            out_shapes.append(jax.ShapeDtypeStruct((n, k), BF16))
        else:
            out_specs.append(pl.BlockSpec((rows, n), lambda *ids, blk=blk: (blk(*ids), 0)))
            out_shapes.append(jax.ShapeDtypeStruct((k, n), BF16))
    return arrays, in_specs, out_specs, out_shapes


def _cast_blocks(src_refs, dst_refs, jobs):
    for src, dst, (_, _, transpose) in zip(src_refs, dst_refs, jobs):
        v = src[...]
        dst[...] = (v.T if transpose else v).astype(BF16)


def _gmlp_in_kernel(*refs, n_stage, jobs):
    nj = len(jobs)
    x_ref, g_ref, w_ref, b_ref, lng_ref, lnb_ref = refs[:6]
    cast_src = refs[6:6 + nj]
    o_ref = refs[6 + nj]
    cast_dst = refs[7 + nj:7 + 2 * nj]
    wb_ref = refs[7 + 2 * nj]
    s = pl.program_id(0)
    width = lng_ref.shape[1]

    @pl.when(s < n_stage)
    def _():
        _stage_rows(s, w_ref, wb_ref)

    @pl.when(s >= n_stage)
    def _():
        _cast_blocks(cast_src, cast_dst, jobs)
        h = _rms(x_ref[...], g_ref[...]).astype(BF16)
        u = _gelu(jnp.dot(h, wb_ref[:, :width], preferred_element_type=F32) + b_ref[:, :width])
        o_ref[:, :width] = u.astype(o_ref.dtype)
        z = _gelu(jnp.dot(h, wb_ref[:, width:], preferred_element_type=F32) + b_ref[:, width:])
        mu = jnp.mean(z, axis=-1, keepdims=True)
        zc = z - mu
        var = jnp.mean(zc * zc, axis=-1, keepdims=True)
        o_ref[:, width:] = (zc * lax.rsqrt(var + EPS) * lng_ref[...] + lnb_ref[...]).astype(o_ref.dtype)


def _gmlp_in(x, g, w_all, layer, b, ln_g, ln_b, jobs):
    n, d = x.shape
    width = w_all.shape[2] // 2
    tm = _tile(n, 512)
    n_tiles = n // tm
    n_stage = d // STAGE_ROWS
    work = lambda s: _work_index(s, n_stage)
    cast_arrays, cast_in, cast_out, cast_shapes = _cast_plan(jobs, n_tiles, work)
    outs = pl.pallas_call(
        functools.partial(_gmlp_in_kernel, n_stage=n_stage, jobs=jobs),
        grid=(n_stage + n_tiles,),
        in_specs=[
            pl.BlockSpec((tm, d), lambda s: (work(s), 0)),
            pl.BlockSpec((1, d), lambda s: (0, 0)),
            pl.BlockSpec((None, STAGE_ROWS, 2 * width), lambda s: (layer, jnp.minimum(s, n_stage - 1), 0)),
            pl.BlockSpec((1, 2 * width), lambda s: (0, 0)),
            pl.BlockSpec((1, width), lambda s: (0, 0)),
            pl.BlockSpec((1, width), lambda s: (0, 0)),
        ] + cast_in,
        out_specs=[pl.BlockSpec((tm, 2 * width), lambda s: (work(s), 0))] + cast_out,
        out_shape=[jax.ShapeDtypeStruct((n, 2 * width), BF16)] + cast_shapes,
        scratch_shapes=[pltpu.VMEM((d, 2 * width), BF16)],
        compiler_params=_params("arbitrary"),
        name="gmlp_in",
    )(x, _row(g), w_all, _row(b), _row(ln_g), _row(ln_b), *cast_arrays)
    return outs[0], outs[1:]


def _gmlp_out_kernel(*refs, jobs):
    nj = len(jobs)
    x_ref, u_ref, v_ref, ws_ref, bs_ref, w_ref = refs[:6]
    o_ref, gated_ref = refs[6 + nj], refs[7 + 2 * nj]
    _cast_blocks(refs[6:6 + nj], refs[7 + nj:7 + 2 * nj], jobs)
    tm, width = u_ref.shape
    heads = ws_ref.shape[0]
    t_idx = lax.broadcasted_iota(jnp.int32, (GMLP_CHUNK, GMLP_CHUNK), 0)
    s_idx = lax.broadcasted_iota(jnp.int32, (GMLP_CHUNK, GMLP_CHUNK), 1)
    causal = t_idx >= s_idx
    for h in range(heads):
        w_h = jnp.where(causal, ws_ref[h], 0.0).astype(BF16)
        b_h = bs_ref[h]
        cols = slice(h * GMLP_HEAD_DIM, (h + 1) * GMLP_HEAD_DIM)
        for c in range(tm // GMLP_CHUNK):
            rows = slice(c * GMLP_CHUNK, (c + 1) * GMLP_CHUNK)
            sg = jnp.dot(w_h, v_ref[rows, cols], preferred_element_type=F32) + b_h
            gated_ref[rows, cols] = (u_ref[rows, cols].astype(F32) * sg).astype(BF16)
    o_ref[...] = x_ref[...] + jnp.dot(gated_ref[...], w_ref[...], preferred_element_type=F32)


def _gmlp_out(x, z, w_s, b_s, w_out, jobs):
    n, d = x.shape
    width = z.shape[1] // 2
    heads = w_s.shape[0]
    tm = _tile(n, 512)
    cast_arrays, cast_in, cast_out, cast_shapes = _cast_plan(jobs, n // tm, lambda i: i)
    outs = pl.pallas_call(
        functools.partial(_gmlp_out_kernel, jobs=jobs),
        grid=(n // tm,),
        in_specs=[
            pl.BlockSpec((tm, d), lambda i: (i, 0)),
            pl.BlockSpec((tm, width), lambda i: (i, 0)),
            pl.BlockSpec((tm, width), lambda i: (i, 1)),
            _once((heads, GMLP_CHUNK, GMLP_CHUNK), lambda i: (0, 0, 0)),
            _once((heads, GMLP_CHUNK, 1), lambda i: (0, 0, 0)),
            _once((width, d), lambda i: (0, 0)),
        ] + cast_in,
        out_specs=[pl.BlockSpec((tm, d), lambda i: (i, 0))] + cast_out,
        out_shape=[jax.ShapeDtypeStruct((n, d), F32)] + cast_shapes,
        scratch_shapes=[pltpu.VMEM((tm, width), BF16)],
        compiler_params=_params("arbitrary"),
        name="gmlp_out",
    )(x, z, z, w_s, b_s.reshape(heads, GMLP_CHUNK, 1), w_out, *cast_arrays)
    return outs[0], outs[1:]


def _mem_kv_kernel(x_ref, g_ref, w_ref, o_ref, h_ref):
    @pl.when(pl.program_id(1) == 0)
    def _():
        h_ref[...] = _rms(x_ref[...], g_ref[...]).astype(BF16)

    o_ref[...] = jnp.dot(h_ref[...], w_ref[...].astype(BF16), preferred_element_type=F32).astype(o_ref.dtype)


def _mem_kv(mem, g_all, w_all):
    n, d = mem.shape
    depth, _, m = w_all.shape
    tn = _tile(m, 1024)
    return pl.pallas_call(
        _mem_kv_kernel,
        grid=(depth, m // tn),
        in_specs=[
            _once((n, d), lambda l, j: (0, 0)),
            pl.BlockSpec((None, 1, d), lambda l, j: (l, 0, 0)),
            pl.BlockSpec((None, d, tn), lambda l, j: (l, 0, j)),
        ],
        out_specs=pl.BlockSpec((None, n, tn), lambda l, j: (l, 0, j)),
        out_shape=jax.ShapeDtypeStruct((depth, n, m), BF16),
        scratch_shapes=[pltpu.VMEM((n, d), BF16)],
        compiler_params=_params("arbitrary", "arbitrary"),
        name="mem_kv",
    )(mem, g_all.reshape(depth, 1, d), w_all)


def _attn_kernel(x_ref, g_ref, wq_ref, kv_ref, wo_ref, o_ref, heads_ref):
    x = x_ref[...]
    d = x.shape[1]
    hd = d // ATTN_HEADS
    q = jnp.dot(_rms(x, g_ref[...]).astype(BF16), wq_ref[...], preferred_element_type=F32).astype(BF16)
    scale = hd ** -0.5
    for h in range(ATTN_HEADS):
        cols = slice(h * hd, (h + 1) * hd)
        k_h = kv_ref[:, h * hd:(h + 1) * hd]
        v_h = kv_ref[:, d + h * hd:d + (h + 1) * hd]
        sc = _nt_dot(q[:, cols], k_h) * scale
        p = jnp.exp(sc - jnp.max(sc, axis=-1, keepdims=True))
        p = p / jnp.sum(p, axis=-1, keepdims=True)
        heads_ref[:, cols] = jnp.dot(p.astype(BF16), v_h, preferred_element_type=F32).astype(BF16)
    o_ref[...] = x + jnp.dot(heads_ref[...], wo_ref[...], preferred_element_type=F32)


def _attention(x, g, w_q, kv, layer, w_o, seq):
    n, d = x.shape
    mem_len = kv.shape[1] // (n // seq)
    tm = _tile(seq, 512)
    per_batch = seq // tm
    return pl.pallas_call(
        _attn_kernel,
        grid=(n // tm,),
        in_specs=[
            pl.BlockSpec((tm, d), lambda i: (i, 0)),
            pl.BlockSpec((1, d), lambda i: (0, 0)),
            _once((d, d), lambda i: (0, 0)),
            pl.BlockSpec((None, mem_len, 2 * d), lambda i: (layer, i // per_batch, 0)),
            _once((d, d), lambda i: (0, 0)),
        ],
        out_specs=pl.BlockSpec((tm, d), lambda i: (i, 0)),
        out_shape=jax.ShapeDtypeStruct((n, d), F32),
        scratch_shapes=[pltpu.VMEM((tm, d), BF16)],
        compiler_params=_params("arbitrary"),
        name="attention",
    )(x, _row(g), w_q, kv, w_o)


def _mlp_kernel(*refs, reorder, final_norm, per_batch, n_tiles, jobs):
    nj = len(jobs)
    x_hbm, g_ref, gf_ref, wup_ref, wdn_ref = refs[:5]
    cast_src = refs[5:5 + nj]
    o_ref = refs[5 + nj]
    cast_dst = refs[6 + nj:6 + 2 * nj]
    xbuf, h_ref, sem = refs[6 + 2 * nj:]
    i, j = pl.program_id(0), pl.program_id(1)
    tm, d = h_ref.shape
    rb = tm // S5_CHUNK

    def x_tile(tile):
        if reorder == "none":
            return [pltpu.make_async_copy(x_hbm.at[pl.ds(tile * tm, tm)], xbuf, sem)]
        if reorder == "to_tmajor":
            return [pltpu.make_async_copy(x_hbm.at[pl.ds(tile * rb, rb), t, :], xbuf.at[t], sem)
                    for t in range(S5_CHUNK)]
        row0 = (tile // per_batch) * S5_CHUNK
        return [pltpu.make_async_copy(x_hbm.at[row0 + t, pl.ds((tile % per_batch) * rb, rb), :], xbuf.at[:, t, :], sem)
                for t in range(S5_CHUNK)]

    @pl.when(j == 0)
    def _():
        @pl.when(i == 0)
        def _():
            for cp in x_tile(0):
                cp.start()

        for cp in x_tile(i):
            cp.wait()
        o_ref[...] = xbuf[...]
        h_ref[...] = _rms(xbuf[...].reshape(tm, d), g_ref[...]).astype(BF16)

        @pl.when(i + 1 < n_tiles)
        def _():
            for cp in x_tile(i + 1):
                cp.start()

    _cast_blocks(cast_src, cast_dst, jobs)
    a = jnp.dot(h_ref[...], wup_ref[...].astype(BF16), preferred_element_type=F32)
    a = jnp.square(jnp.maximum(a, 0.0)).astype(BF16)
    o_ref[...] += jnp.dot(a, wdn_ref[...].astype(BF16), preferred_element_type=F32).reshape(o_ref.shape)

    if final_norm:
        @pl.when(j == pl.num_programs(1) - 1)
        def _():
            o_ref[...] = _rms(o_ref[...], gf_ref[...])


def _mlp(x, seq, g, wup_all, wdn_all, layer, tmajor_in, tmajor_out, g_final=None, jobs=()):
    n, d = x.shape
    ff = wup_all.shape[2]
    tm = _tile(seq, 1024)
    tf = _tile(ff, 512)
    per_batch = seq // tm
    n_tiles, n_ff = n // tm, ff // tf
    rb = tm // S5_CHUNK
    r = seq // S5_CHUNK
    nat_out = ((n // S5_CHUNK, S5_CHUNK, d), (rb, S5_CHUNK, d), lambda i, j: (i, 0, 0))
    tmj_out = ((n // seq, S5_CHUNK, r, d), (None, S5_CHUNK, rb, d), lambda i, j: (i // per_batch, 0, i % per_batch, 0))
    flat_out = ((n, d), (tm, d), lambda i, j: (i, 0))
    if tmajor_in == tmajor_out:
        reorder, src_view, dst = "none", (n, d), flat_out
    elif tmajor_out:
        reorder, src_view, dst = "to_tmajor", (n // S5_CHUNK, S5_CHUNK, d), tmj_out
    else:
        reorder, src_view, dst = "from_tmajor", (n // seq * S5_CHUNK, r, d), nat_out
    tile_shape = tuple(s for s in dst[1] if s is not None)
    gf = g if g_final is None else g_final
    cast_arrays, cast_in, cast_out, cast_shapes = _cast_plan(jobs, n_tiles * n_ff, lambda i, j: i * n_ff + j)
    outs = pl.pallas_call(
        functools.partial(_mlp_kernel, reorder=reorder, final_norm=g_final is not None,
                          per_batch=per_batch, n_tiles=n_tiles, jobs=jobs),
        grid=(n_tiles, n_ff),
        in_specs=[
            pl.BlockSpec(memory_space=pl.ANY),
            pl.BlockSpec((1, d), lambda i, j: (0, 0)),
            pl.BlockSpec((1, d), lambda i, j: (0, 0)),
            pl.BlockSpec((None, d, tf), lambda i, j: (layer, 0, j)),
            pl.BlockSpec((None, tf, d), lambda i, j: (layer, j, 0)),
        ] + cast_in,
        out_specs=[pl.BlockSpec(dst[1], dst[2])] + cast_out,
        out_shape=[jax.ShapeDtypeStruct(dst[0], F32)] + cast_shapes,
        scratch_shapes=[pltpu.VMEM(tile_shape, F32), pltpu.VMEM((tm, d), BF16), pltpu.SemaphoreType.DMA(())],
        compiler_params=_params("arbitrary", "arbitrary"),
        name="mlp",
    )(x.reshape(src_view), _row(g), _row(gf), wup_all, wdn_all, *cast_arrays)
    return outs[0].reshape(n, d), outs[1:]


def _s5_in_kernel(x_ref, g_ref, wt_ref, o_ref):
    nt, r, d = x_ref.shape
    groups = o_ref.shape[0]
    h = _rms(x_ref[...].reshape(nt * r, d), g_ref[...]).astype(BF16)
    ut = _nt_dot(wt_ref[...], h).astype(BF16)
    for t in range(nt):
        o_ref[:, t] = ut[:, t * r:(t + 1) * r].reshape(groups, S5_GROUP, r)


def _s5_in(xt, g, w_in_t):
    bsz, _, r, d = xt.shape
    width = w_in_t.shape[0]
    groups = width // S5_GROUP
    return pl.pallas_call(
        _s5_in_kernel,
        grid=(bsz, S5_CHUNK // S5_STEPS),
        in_specs=[
            pl.BlockSpec((None, S5_STEPS, r, d), lambda b, t: (b, t, 0, 0)),
            pl.BlockSpec((1, d), lambda b, t: (0, 0)),
            _once((width, d), lambda b, t: (0, 0)),
        ],
        out_specs=pl.BlockSpec((None, groups, S5_STEPS, S5_GROUP, r), lambda b, t: (b, 0, t, 0, 0)),
        out_shape=jax.ShapeDtypeStruct((bsz, groups, S5_CHUNK, S5_GROUP, r), BF16),
        compiler_params=_params("arbitrary", "arbitrary"),
        name="s5_in",
    )(xt, _row(g), w_in_t)


def _cmul(ar, ai, br, bi):
    return ar * br - ai * bi, ar * bi + ai * br


def _s5_prep_kernel(lr_ref, li_ref, ldt_ref, bre_ref, bim_ref, cre_ref, cim_ref, crp_ref, cip_ref, dsk_ref,
                    t_ref, wb_ref, wcr_ref, wci_ref, dre_ref, dim_ref):
    npair = lr_ref.shape[0]
    idx = lambda shape, dim: lax.broadcasted_iota(jnp.int32, shape, dim)
    rev_t = (idx((S5_PAIR, S5_BLOCK), 0) == S5_CHUNK - 1 - idx((S5_PAIR, S5_BLOCK), 1) // S5_GROUP).astype(BF16)
    tile_c = (idx((S5_GROUP, S5_BLOCK), 0) == idx((S5_GROUP, S5_BLOCK), 1) % S5_GROUP).astype(BF16)
    skip_mask = ((idx((S5_GROUP, S5_BLOCK), 1) // S5_GROUP == S5_CHUNK - 1)
                 & (idx((S5_GROUP, S5_BLOCK), 1) % S5_GROUP == idx((S5_GROUP, S5_BLOCK), 0)))
    k_col = idx((S5_POW_ROWS, 1), 0).astype(F32)
    pad_rows = jnp.zeros((S5_PAIR - S5_POW_ROWS, S5_PAIR), F32)

    def split_dot(x, m):
        hi = x.astype(BF16)
        lo = (x - hi.astype(F32)).astype(BF16)
        return jnp.dot(hi, m, preferred_element_type=F32) + jnp.dot(lo, m, preferred_element_type=F32)

    def rows_x16(tab):
        return jnp.concatenate([jnp.broadcast_to(tab[t + 1:t + 2], (S5_GROUP, S5_PAIR)) for t in range(S5_CHUNK)],
                               axis=0)

    for p in range(npair):
        lr = jnp.minimum(lr_ref[p], LAM_RE_MAX)
        li = li_ref[p]
        dt = jnp.exp(ldt_ref[p])
        zr, zi = lr * dt, li * dt
        mag = jnp.exp(k_col * zr)
        pw_r, pw_i = mag * jnp.cos(k_col * zi), mag * jnp.sin(k_col * zi)
        ar, ai = pw_r[1:2], pw_i[1:2]
        den = lr * lr + li * li
        qr = ((ar - 1.0) * lr + ai * li) / den
        qi = (ai * lr - (ar - 1.0) * li) / den
        dre_ref[p] = pw_r[S5_CHUNK:S5_CHUNK + 1]
        dim_ref[p] = pw_i[S5_CHUNK:S5_CHUNK + 1]
        nx_r, nx_i = rows_x16(pw_r), rows_x16(pw_i)
        sc_r, sc_i = _cmul(pw_r, pw_i, qr, qi)
        sp_r = split_dot(jnp.concatenate([sc_r, pad_rows], axis=0).T, rev_t)
        sp_i = split_dot(jnp.concatenate([sc_i, pad_rows], axis=0).T, rev_t)
        wb_r, wb_i = _cmul(sp_r, sp_i, split_dot(bre_ref[p], tile_c), split_dot(bim_ref[p], tile_c))
        for k in range(2):
            g = 2 * p + k
            rows = slice(k * S5_STATE, (k + 1) * S5_STATE)
            wb_g = jnp.concatenate([wb_r[rows], wb_i[rows]], axis=0).astype(BF16)
            wb_ref[g] = wb_g
            krev = split_dot(cre_ref[g], wb_g[:S5_STATE]) - split_dot(cim_ref[g], wb_g[S5_STATE:])
            krev = krev + jnp.where(skip_mask, dsk_ref[g], 0.0)
            kext = jnp.concatenate([krev, jnp.zeros_like(krev)], axis=1)
            for tp in range(S5_CHUNK):
                off = (S5_CHUNK - 1 - tp) * S5_GROUP
                win = kext if off == 0 else pltpu.roll(kext, 2 * S5_BLOCK - off, axis=1)
                t_ref[g, tp * S5_GROUP:(tp + 1) * S5_GROUP, :] = win[:, :S5_BLOCK].astype(BF16)
            c_r = jnp.concatenate([crp_ref[g]] * S5_CHUNK, axis=0)
            c_i = jnp.concatenate([cip_ref[g]] * S5_CHUNK, axis=0)
            wcr_ref[g] = (c_r * nx_r - c_i * nx_i).astype(BF16)
            wci_ref[g] = (-(c_r * nx_i + c_i * nx_r)).astype(BF16)


def _s5_prep(lam_re, lam_im, log_dt, bm_re, bm_im, cm_re, cm_im, d_skip):
    groups, state = lam_re.shape
    assert state == S5_STATE and groups % 2 == 0
    gb = _tile(groups, 8)
    pairs = lambda a: a.reshape(groups // 2, 1, S5_PAIR)
    half = jnp.eye(2, dtype=F32)[jnp.arange(groups) % 2]
    cpad = lambda a: (a[:, :, None, :] * half[:, None, :, None]).reshape(groups, S5_GROUP, S5_PAIR)
    g3 = lambda n, shape: pl.BlockSpec((n,) + shape, lambda i: (i, 0, 0))
    return pl.pallas_call(
        _s5_prep_kernel,
        grid=(groups // gb,),
        in_specs=[g3(gb // 2, (1, S5_PAIR)), g3(gb // 2, (1, S5_PAIR)), g3(gb // 2, (1, S5_PAIR)),
                  g3(gb // 2, (S5_PAIR, S5_GROUP)), g3(gb // 2, (S5_PAIR, S5_GROUP)),
                  g3(gb, (S5_GROUP, state)), g3(gb, (S5_GROUP, state)),
                  g3(gb, (S5_GROUP, S5_PAIR)), g3(gb, (S5_GROUP, S5_PAIR)), g3(gb, (1, S5_BLOCK))],
        out_specs=[g3(gb, (S5_BLOCK, S5_BLOCK)), g3(gb, (S5_PAIR, S5_BLOCK)), g3(gb, (S5_BLOCK, S5_PAIR)),
                   g3(gb, (S5_BLOCK, S5_PAIR)), g3(gb // 2, (1, S5_PAIR)), g3(gb // 2, (1, S5_PAIR))],
        out_shape=[jax.ShapeDtypeStruct((groups, S5_BLOCK, S5_BLOCK), BF16),
                   jax.ShapeDtypeStruct((groups, S5_PAIR, S5_BLOCK), BF16),
                   jax.ShapeDtypeStruct((groups, S5_BLOCK, S5_PAIR), BF16),
                   jax.ShapeDtypeStruct((groups, S5_BLOCK, S5_PAIR), BF16),
                   jax.ShapeDtypeStruct((groups // 2, 1, S5_PAIR), F32),
                   jax.ShapeDtypeStruct((groups // 2, 1, S5_PAIR), F32)],
        compiler_params=_params("arbitrary"),
        name="s5_prep",
    )(pairs(lam_re), pairs(lam_im), pairs(jnp.repeat(log_dt, state)),
      bm_re.reshape(groups // 2, S5_PAIR, S5_GROUP), bm_im.reshape(groups // 2, S5_PAIR, S5_GROUP),
      cm_re, cm_im, cpad(cm_re), cpad(cm_im), jnp.tile(d_skip.reshape(groups, 1, S5_GROUP), (1, 1, S5_CHUNK)))


def _s5_core_kernel(ut_ref, t_ref, wb_ref, wcr_ref, wci_ref, dre_ref, dim_ref, o_ref,
                    elr_ref, eli_ref, epr_ref, epi_ref):
    bsz, gb, _, r = ut_ref.shape
    for b in range(bsz):
        for p in range(gb // 2):
            parts = [jnp.dot(wb_ref[2 * p + k], ut_ref[b, 2 * p + k], preferred_element_type=F32) for k in range(2)]
            lanes = slice(p * S5_PAIR, (p + 1) * S5_PAIR)
            elr_ref[b, :, lanes] = jnp.concatenate([parts[0][:S5_STATE], parts[1][:S5_STATE]], axis=0).T
            eli_ref[b, :, lanes] = jnp.concatenate([parts[0][S5_STATE:], parts[1][S5_STATE:]], axis=0).T
    lanes_all = gb * S5_STATE
    d1 = (jnp.concatenate([dre_ref[p] for p in range(gb // 2)], axis=1),
          jnp.concatenate([dim_ref[p] for p in range(gb // 2)], axis=1))
    d2 = _cmul(*d1, *d1)
    d4 = _cmul(*d2, *d2)
    d6 = _cmul(*d4, *d2)
    pows = [d1, d2, _cmul(*d2, *d1), d4, _cmul(*d4, *d1), d6, _cmul(*d6, *d1), _cmul(*d4, *d4)]
    dp_r = jnp.concatenate([p[0] for p in pows], axis=0)
    dp_i = jnp.concatenate([p[1] for p in pows], axis=0)
    sub = lax.broadcasted_iota(jnp.int32, (SUBLANES, lanes_all), 0)

    def shift_rows(x, n, fill):
        return jnp.where(sub >= n, pltpu.roll(x, n, axis=0), fill)

    def tile_step(k, carry):
        rows = pl.ds(pl.multiple_of(k * SUBLANES, SUBLANES), SUBLANES)
        out = []
        for b in range(bsz):
            cr, ci = carry[b]
            lr, li = elr_ref[b, rows, :], eli_ref[b, rows, :]
            for n, (pr, pi) in ((1, d1), (2, d2), (4, d4)):
                sr, si = shift_rows(lr, n, 0.0), shift_rows(li, n, 0.0)
                lr, li = lr + pr * sr - pi * si, li + pr * si + pi * sr
            er = lr + dp_r * cr - dp_i * ci
            ei = li + dp_r * ci + dp_i * cr
            epr_ref[b, rows, :] = shift_rows(er, 1, cr)
            epi_ref[b, rows, :] = shift_rows(ei, 1, ci)
            out.append((jnp.broadcast_to(er[SUBLANES - 1:], er.shape), jnp.broadcast_to(ei[SUBLANES - 1:], ei.shape)))
        return tuple(out)

    zero = jnp.zeros((SUBLANES, lanes_all), F32)
    lax.fori_loop(0, r // SUBLANES, tile_step, tuple((zero, zero) for _ in range(bsz)))
    for b in range(bsz):
        ys = []
        for g in range(gb):
            lanes = slice((g // 2) * S5_PAIR, (g // 2 + 1) * S5_PAIR)
            y = jnp.dot(t_ref[g], ut_ref[b, g], preferred_element_type=F32)
            y = y + _nt_dot(wcr_ref[g], epr_ref[b, :, lanes].astype(BF16))
            y = y + _nt_dot(wci_ref[g], epi_ref[b, :, lanes].astype(BF16))
            ys.append(_gelu(y))
        for t in range(S5_CHUNK):
            rows = slice(t * S5_GROUP, (t + 1) * S5_GROUP)
            o_ref[b, t] = jnp.concatenate([y[rows] for y in ys], axis=0).T.astype(BF16)


def _s5_core(ut, tmat, wb, wcr, wci, dre, dim):
    bsz, groups, blk, r = ut.shape
    gb = _tile(groups, 8)
    g3 = lambda n, shape: pl.BlockSpec((n,) + shape, lambda i: (i, 0, 0))
    return pl.pallas_call(
        _s5_core_kernel,
        grid=(groups // gb,),
        in_specs=[pl.BlockSpec((bsz, gb, blk, r), lambda i: (0, i, 0, 0)),
                  g3(gb, (blk, blk)), g3(gb, (S5_PAIR, blk)), g3(gb, (blk, S5_PAIR)), g3(gb, (blk, S5_PAIR)),
                  g3(gb // 2, (1, S5_PAIR)), g3(gb // 2, (1, S5_PAIR))],
        out_specs=pl.BlockSpec((bsz, S5_CHUNK, r, gb * S5_GROUP), lambda i: (0, 0, 0, i)),
        out_shape=jax.ShapeDtypeStruct((bsz, S5_CHUNK, r, groups * S5_GROUP), BF16),
        scratch_shapes=[pltpu.VMEM((bsz, r, gb * S5_STATE), F32) for _ in range(4)],
        compiler_params=_params("arbitrary"),
        name="s5_core",
    )(ut, tmat, wb, wcr, wci, dre, dim)


def _s5_out_kernel(x_ref, y_ref, wv_ref, wg_ref, bv_ref, bg_ref, o_ref):
    nt, r, width = y_ref.shape
    y = y_ref[...].reshape(nt * r, width)
    val = jnp.dot(y, wv_ref[...], preferred_element_type=F32) + bv_ref[...]
    gate = jnp.dot(y, wg_ref[...], preferred_element_type=F32) + bg_ref[...]
    o_ref[...] = x_ref[...] + (val * jax.nn.sigmoid(gate)).reshape(o_ref.shape)


def _s5_out(xt, yt, w_out, b_out):
    bsz, _, r, d = xt.shape
    width = w_out.shape[0]
    b2 = b_out.reshape(1, 2 * d)
    return pl.pallas_call(
        _s5_out_kernel,
        grid=(bsz, S5_CHUNK // S5_STEPS),
        in_specs=[
            pl.BlockSpec((None, S5_STEPS, r, d), lambda b, t: (b, t, 0, 0)),
            pl.BlockSpec((None, S5_STEPS, r, width), lambda b, t: (b, t, 0, 0)),
            _once((width, d), lambda b, t: (0, 0)),
            _once((width, d), lambda b, t: (0, 1)),
            pl.BlockSpec((1, d), lambda b, t: (0, 0)),
            pl.BlockSpec((1, d), lambda b, t: (0, 1)),
        ],
        out_specs=pl.BlockSpec((None, S5_STEPS, r, d), lambda b, t: (b, t, 0, 0)),
        out_shape=jax.ShapeDtypeStruct(xt.shape, F32),
        compiler_params=_params("arbitrary", "arbitrary"),
        name="s5_out",
    )(xt, yt, w_out, w_out, b2, b2)


def _s5_mixer(x, seq, g, w_in_t, lam_re, lam_im, log_dt, bm_re, bm_im, cm_re, cm_im, d_skip, w_out, b_out):
    n, d = x.shape
    bsz = n // seq
    r = seq // S5_CHUNK
    groups = w_in_t.shape[0] // S5_GROUP
    xt = x.reshape(bsz, S5_CHUNK, r, d)
    ut = _s5_in(xt, g, w_in_t)
    tmat, wb, wcr, wci, dre, dim = _s5_prep(lam_re, lam_im, log_dt, bm_re, bm_im, cm_re, cm_im, d_skip)
    y = _s5_core(ut.reshape(bsz, groups, S5_BLOCK, r), tmat, wb, wcr, wci, dre, dim)
    out = _s5_out(xt, y, w_out, b_out)
    return out.reshape(n, d)


def kernel(x, mem, g_mix, g_xattn, g_mem, g_ff, g_final, a_w_in, a_b_in, a_ln_g, a_ln_b, a_w_s, a_b_s, a_w_out, b_w_in, b_lam_re, b_lam_im, b_log_dt, b_bm_re, b_bm_im, b_cm_re, b_cm_im, b_d, b_w_out, b_b_out, x_w_q, x_w_kv, x_w_o, f_w_up, f_w_down):
    bsz, seq, d = x.shape
    depth = g_mix.shape[0]
    assert seq % GMLP_CHUNK == 0 and seq % S5_CHUNK == 0 and d % ATTN_HEADS == 0
    h = x.reshape(bsz * seq, d)
    mem2 = mem.reshape(-1, d)
    tmajor = False
    attn_jobs = lambda i: [(x_w_q, i, False), (x_w_o, i, False)]
    ready = ()
    kv = _mem_kv(mem2, g_mem, x_w_kv)
    for i in range(depth):
        j = i // N_MIXERS
        if i % N_MIXERS == 0:
            assert not tmajor
            z, (w_out,) = _gmlp_in(h, g_mix[i], a_w_in, j, a_b_in[j], a_ln_g[j], a_ln_b[j], [(a_w_out, j, False)])
            h, (w_q, w_o) = _gmlp_out(h, z, a_w_s[j], a_b_s[j], w_out, attn_jobs(i))
        else:
            assert tmajor
            w_in_t, w_out, w_q, w_o = ready
            h = _s5_mixer(h, seq, g_mix[i], w_in_t, b_lam_re[j], b_lam_im[j], b_log_dt[j], b_bm_re[j],
                          b_bm_im[j], b_cm_re[j], b_cm_im[j], b_d[j], w_out, b_b_out[j])
        h = _attention(h, g_xattn[i], w_q, kv, i, w_o, seq)
        last = i == depth - 1
        next_is_s5 = (not last) and (i + 1) % N_MIXERS == 1
        jobs = [(b_w_in, (i + 1) // N_MIXERS, True), (b_w_out, (i + 1) // N_MIXERS, False)] + attn_jobs(i + 1) \
            if next_is_s5 else []
        h, ready = _mlp(h, seq, g_ff[i], f_w_up, f_w_down, i, tmajor, next_is_s5,
                        g_final if last else None, jobs)
        tmajor = next_is_s5
    return h.reshape(bsz, seq, d)
```
